```python
import jax, jax.numpy as jnp
from jax import lax
import numpy as np

D_MODEL = 2048
BATCH = 1
SEQ = 8192
DEPTH = 2
DEC_BATCH = 4
DEC_SEQ = 4096
PAST_LEN = 128

ATTN_HEAD_DIM = 128
ATTN_WIDTH = D_MODEL // 2
ATTN_HEADS = ATTN_WIDTH // ATTN_HEAD_DIM
DILATED_PATTERNS = ((128, 1), (512, 4), (2048, 16))
ATTN_BLOCK = 64
RET_WIDTH = D_MODEL - ATTN_WIDTH
RET_HEADS = 4
RET_KEY_DIM = RET_WIDTH // RET_HEADS
RET_VAL_DIM = RET_WIDTH // RET_HEADS
RET_CHUNK = 128
MIX_WIDTH = ATTN_WIDTH + RET_WIDTH
D_FF = ((8 * D_MODEL // 3 + 127) // 128) * 128
ROPE_THETA = 10000.0
NORM_EPS = 1e-6
NEG_INF = -1e30
IN_SIZES = (ATTN_WIDTH, ATTN_WIDTH, ATTN_WIDTH,
            RET_HEADS * RET_KEY_DIM, RET_HEADS * RET_KEY_DIM, RET_WIDTH, RET_WIDTH)
IN_COLS = sum(IN_SIZES)
IN_SPLITS = tuple(sum(IN_SIZES[:i + 1]) for i in range(len(IN_SIZES) - 1))

kernel_name = "hybrid_dilated_attn_retention_macaron_encoder"


def rms_norm(x, gain):
    xf = x.astype(jnp.float32)
    y = xf * lax.rsqrt(jnp.mean(xf * xf, axis=-1, keepdims=True) + NORM_EPS)
    return (y * gain.astype(jnp.float32)).astype(x.dtype)


def swiglu(x, w_gate, w_up, w_down):
    return (jax.nn.silu(x @ w_gate) * (x @ w_up)) @ w_down


def apply_rope(x):
    S, Dh = x.shape[1], x.shape[-1]
    half = Dh // 2
    inv_freq = ROPE_THETA ** (-jnp.arange(half, dtype=jnp.float32) / half)
    ang = jnp.arange(S, dtype=jnp.float32)[:, None] * inv_freq[None, :]
    cos = jnp.cos(ang)[None, :, None, :]
    sin = jnp.sin(ang)[None, :, None, :]
    x1, x2 = x[..., :half], x[..., half:]
    return jnp.concatenate([x1 * cos - x2 * sin, x2 * cos + x1 * sin], axis=-1)


def dilated_window_attention(q, k, v, dilation, half):
    B, S, H, Dh = q.shape
    L = S // dilation
    nb = -(-L // ATTN_BLOCK)
    Lp = nb * ATTN_BLOCK
    Z = B * dilation

    def to_sub(t):
        t = t.reshape(B, L, dilation, H, Dh).transpose(0, 2, 1, 3, 4)
        return t.reshape(Z, L, H, Dh)

    def windows(t):
        t = jnp.pad(to_sub(t), ((0, 0), (ATTN_BLOCK, Lp - L + ATTN_BLOCK), (0, 0), (0, 0)))
        t = t.reshape(Z, nb + 2, ATTN_BLOCK, H, Dh)
        return jnp.concatenate([t[:, :-2], t[:, 1:-1], t[:, 2:]], axis=2)

    qs = jnp.pad(to_sub(q), ((0, 0), (0, Lp - L), (0, 0), (0, 0))).reshape(Z, nb, ATTN_BLOCK, H, Dh)
    kw = windows(k)
    vw = windows(v)
    blk = jnp.arange(nb)[:, None] * ATTN_BLOCK
    qpos = blk + jnp.arange(ATTN_BLOCK)[None, :]
    kpos = blk - ATTN_BLOCK + jnp.arange(3 * ATTN_BLOCK)[None, :]
    valid = ((jnp.abs(qpos[:, :, None] - kpos[:, None, :]) <= half)
             & (kpos[:, None, :] >= 0) & (kpos[:, None, :] < L))
    scores = jnp.einsum('znqhd,znkhd->znhqk', qs, kw) * (Dh ** -0.5)
    scores = jnp.where(valid[None, :, None], scores, NEG_INF)
    m = jnp.max(scores, axis=-1, keepdims=True)
    p = jnp.exp(scores - m)
    den = jnp.sum(p, axis=-1)
    out = jnp.einsum('znhqk,znkhd->znqhd', p, vw) / jnp.swapaxes(den, 2, 3)[..., None]
    lse = jnp.swapaxes(m[..., 0] + jnp.log(den), 2, 3)

    def from_sub(t):
        t = t.reshape((Z, Lp) + t.shape[3:])[:, :L]
        t = t.reshape((B, dilation, L) + t.shape[2:])
        return jnp.swapaxes(t, 1, 2).reshape((B, S) + t.shape[3:])

    return from_sub(out), from_sub(lse)


def dilated_attention_mixer(q, k, v, q_gain, k_gain):
    B, S, _ = q.shape
    shp = (B, S, ATTN_HEADS, ATTN_HEAD_DIM)
    q = apply_rope(rms_norm(q.reshape(shp).astype(jnp.float32), q_gain))
    k = apply_rope(rms_norm(k.reshape(shp).astype(jnp.float32), k_gain))
    v = v.reshape(shp).astype(jnp.float32)
    outs, lses = [], []
    for window, dilation in DILATED_PATTERNS:
        o, l = dilated_window_attention(q, k, v, dilation, window // (2 * dilation))
        outs.append(o)
        lses.append(l)
    weights = jax.nn.softmax(jnp.stack(lses), axis=0)
    out = jnp.sum(weights[..., None] * jnp.stack(outs), axis=0)
    return out.reshape(B, S, ATTN_WIDTH)


def retention_scan(q, k, v, log_decay):
    B, S, H, dk = q.shape
    dv = v.shape[-1]
    C = RET_CHUNK
    nC = S // C
    qc = q.reshape(B, nC, C, H, dk)
    kc = k.reshape(B, nC, C, H, dk)
    vc = v.reshape(B, nC, C, H, dv)
    idx = jnp.arange(C)
    diff = idx[:, None] - idx[None, :]
    lg = log_decay[:, None, None]
    dmask = jnp.exp(lg * jnp.maximum(diff, 0)[None]) * (diff >= 0)[None]
    inner = jnp.einsum('bnhcs,bnshe->bnche',
                       jnp.einsum('bnchd,bnshd->bnhcs', qc, kc) * dmask[None, None], vc)
    zeta = jnp.exp(log_decay[:, None] * (C - 1 - idx)[None, :])
    xi = jnp.exp(log_decay[:, None] * (idx + 1)[None, :])
    chunk_kv = jnp.einsum('bnshd,bnshe,hs->nbhde', kc, vc, zeta)
    chunk_decay = jnp.exp(log_decay * C)[None, :, None, None]

    def step(state, kv_n):
        return chunk_decay * state + kv_n, state

    _, prev_state = lax.scan(step, jnp.zeros((B, H, dk, dv), jnp.float32), chunk_kv)
    cross = jnp.einsum('bnchd,nbhde,hc->bnche', qc, prev_state, xi)
    return (inner + cross).reshape(B, S, H, dv)


def retention_mixer(q, k, v, g, log_decay_fwd, log_decay_bwd, out_gain):
    B, S, _ = q.shape
    q = apply_rope(q.reshape(B, S, RET_HEADS, RET_KEY_DIM).astype(jnp.float32))
    k = apply_rope(k.reshape(B, S, RET_HEADS, RET_KEY_DIM).astype(jnp.float32)) * (RET_KEY_DIM ** -0.5)
    v = v.reshape(B, S, RET_HEADS, RET_VAL_DIM).astype(jnp.float32)
    fwd = retention_scan(q, k, v, log_decay_fwd.astype(jnp.float32))
    bwd = jnp.flip(retention_scan(jnp.flip(q, 1), jnp.flip(k, 1), jnp.flip(v, 1),
                                  log_decay_bwd.astype(jnp.float32)), 1)
    out = rms_norm(fwd + bwd, out_gain).reshape(B, S, RET_WIDTH)
    return jax.nn.silu(g.astype(jnp.float32)) * out


def encoder_layer(x, l, params):
    (ffn1_norm, ffn1_w_gate, ffn1_w_up, ffn1_w_down, mix_norm, w_in, attn_q_norm, attn_k_norm,
     ret_log_decay_fwd, ret_log_decay_bwd, ret_out_norm, w_out,
     ffn2_norm, ffn2_w_gate, ffn2_w_up, ffn2_w_down) = params
    x = x + 0.5 * swiglu(rms_norm(x, ffn1_norm[l]), ffn1_w_gate[l], ffn1_w_up[l], ffn1_w_down[l])
    h = rms_norm(x, mix_norm[l])
    aq, ak, av, rq, rk, rv, rg = jnp.split(h @ w_in[l], IN_SPLITS, axis=-1)
    a = dilated_attention_mixer(aq, ak, av, attn_q_norm[l], attn_k_norm[l])
    r = retention_mixer(rq, rk, rv, rg, ret_log_decay_fwd[l], ret_log_decay_bwd[l], ret_out_norm[l])
    x = x + jnp.concatenate([a, r], axis=-1).astype(x.dtype) @ w_out[l]
    x = x + 0.5 * swiglu(rms_norm(x, ffn2_norm[l]), ffn2_w_gate[l], ffn2_w_up[l], ffn2_w_down[l])
    return x


def run_trunk(x, params):
    for l in range(DEPTH):
        x = encoder_layer(x, l, params)
    return x


def setup_inputs(seed: int = 0) -> dict:
    key = jax.random.key(seed)
    ks = jax.random.split(key, 20)
    f32 = jnp.float32

    def normal(k, shape, scale):
        return jax.random.normal(k, shape, f32) * scale

    def gain(k, shape):
        return 1.0 + 0.02 * jax.random.normal(k, shape, f32)

    base_log_decay = jnp.log(1.0 - 2.0 ** (-5.0 - jnp.arange(RET_HEADS, dtype=f32)))

    def log_decay(k):
        return base_log_decay[None, :] * (1.0 + 0.1 * jax.random.uniform(k, (DEPTH, RET_HEADS), f32, -1.0, 1.0))

    return {
        "x_prompt": jax.random.normal(ks[0], (BATCH, SEQ, D_MODEL), f32),
        "x_sample": jax.random.normal(ks[1], (DEC_BATCH, DEC_SEQ, D_MODEL), f32),
        "ffn1_norm": gain(ks[2], (DEPTH, D_MODEL)),
        "ffn1_w_gate": normal(ks[3], (DEPTH, D_MODEL, D_FF), D_MODEL ** -0.5),
        "ffn1_w_up": normal(ks[4], (DEPTH, D_MODEL, D_FF), D_MODEL ** -0.5),
        "ffn1_w_down": normal(ks[5], (DEPTH, D_FF, D_MODEL), D_FF ** -0.5),
        "mix_norm": gain(ks[6], (DEPTH, D_MODEL)),
        "w_in": normal(ks[7], (DEPTH, D_MODEL, IN_COLS), D_MODEL ** -0.5),
        "attn_q_norm": gain(ks[8], (DEPTH, ATTN_HEAD_DIM)),
        "attn_k_norm": gain(ks[9], (DEPTH, ATTN_HEAD_DIM)),
        "ret_log_decay_fwd": log_decay(ks[10]),
        "ret_log_decay_bwd": log_decay(ks[11]),
        "ret_out_norm": gain(ks[12], (DEPTH, RET_HEADS, RET_VAL_DIM)),
        "w_out": normal(ks[13], (DEPTH, MIX_WIDTH, D_MODEL), MIX_WIDTH ** -0.5),
        "ffn2_norm": gain(ks[14], (DEPTH, D_MODEL)),
        "ffn2_w_gate": normal(ks[15], (DEPTH, D_MODEL, D_FF), D_MODEL ** -0.5),
        "ffn2_w_up": normal(ks[16], (DEPTH, D_MODEL, D_FF), D_MODEL ** -0.5),
        "ffn2_w_down": normal(ks[17], (DEPTH, D_FF, D_MODEL), D_FF ** -0.5),
    }


def reference(x_prompt, x_sample, ffn1_norm, ffn1_w_gate, ffn1_w_up, ffn1_w_down, mix_norm, w_in,
              attn_q_norm, attn_k_norm, ret_log_decay_fwd, ret_log_decay_bwd, ret_out_norm, w_out,
              ffn2_norm, ffn2_w_gate, ffn2_w_up, ffn2_w_down):
    params = (ffn1_norm, ffn1_w_gate, ffn1_w_up, ffn1_w_down, mix_norm, w_in, attn_q_norm, attn_k_norm,
              ret_log_decay_fwd, ret_log_decay_bwd, ret_out_norm, w_out,
              ffn2_norm, ffn2_w_gate, ffn2_w_up, ffn2_w_down)
    y_prompt = run_trunk(x_prompt, params)
    y_sample = run_trunk(x_sample, params)
    return (y_prompt, y_sample)
```

```python
import functools

import jax
import jax.numpy as jnp
from jax import lax
from jax.experimental import pallas as pl
from jax.experimental.pallas import tpu as pltpu

F32 = jnp.float32
BF16 = jnp.bfloat16

D_MODEL = 2048
ATTN_HEAD_DIM = 128
ATTN_HEADS = 8
ATTN_WIDTH = ATTN_HEADS * ATTN_HEAD_DIM
DILATIONS = (1, 4, 16)
ATTN_HALF = 64
RET_HEADS = 4
RET_DIM = 256
RET_WIDTH = RET_HEADS * RET_DIM
RET_CHUNK = 128
D_FF = 5504
IN_COLS = 3 * ATTN_WIDTH + 4 * RET_WIDTH
ROPE_THETA = 10000.0
NORM_EPS = 1e-6
NEG_INF = -1e30

V7X_LANES = 128
FF_TILE = 512
D_FF_PAD = -(-D_FF // FF_TILE) * FF_TILE
ROW_TILE = 512
SEG = 1024
ATTN_QBLK = 128
ATTN_KWIN = 256
ATTN_SUPER = ATTN_QBLK * DILATIONS[-1]
RET_ROWS = 512
MIB = 1024 * 1024


def _params(semantics, vmem_bytes):
    return pltpu.CompilerParams(dimension_semantics=semantics, vmem_limit_bytes=int(vmem_bytes))


def _rms_norm_rows(x, gain):
    return x * lax.rsqrt(jnp.mean(x * x, axis=-1, keepdims=True) + NORM_EPS) * gain


def _ffn_kernel(x_ref, g_ref, wg_ref, wu_ref, wd_ref, o_ref, xn_ref):
    f = pl.program_id(1)

    @pl.when(f == 0)
    def _():
        xn_ref[...] = _rms_norm_rows(x_ref[...], g_ref[...]).astype(BF16)
        o_ref[...] = jnp.zeros_like(o_ref)

    xn = xn_ref[...]
    gate = jnp.dot(xn, wg_ref[...], preferred_element_type=F32)
    up = jnp.dot(xn, wu_ref[...], preferred_element_type=F32)
    h = (jax.nn.silu(gate) * up).astype(BF16)
    o_ref[...] += jnp.dot(h, wd_ref[...], preferred_element_type=F32)

    @pl.when(f == pl.num_programs(1) - 1)
    def _():
        o_ref[...] = x_ref[...] + 0.5 * o_ref[...]


def _ffn(x, gain, wg, wu, wd):
    T, D = x.shape
    tm, tf = ROW_TILE, FF_TILE
    vmem = 2 * (2 * tm * D * 4) + tm * D * 2 + 2 * 3 * D * tf * 2 + 4 * tm * tf * 4 + 4 * MIB
    return pl.pallas_call(
        _ffn_kernel,
        grid=(T // tm, D_FF_PAD // tf),
        in_specs=[
            pl.BlockSpec((tm, D), lambda i, f: (i, 0)),
            pl.BlockSpec((1, D), lambda i, f: (0, 0)),
            pl.BlockSpec((D, tf), lambda i, f: (0, f)),
            pl.BlockSpec((D, tf), lambda i, f: (0, f)),
            pl.BlockSpec((tf, D), lambda i, f: (f, 0)),
        ],
        out_specs=pl.BlockSpec((tm, D), lambda i, f: (i, 0)),
        out_shape=jax.ShapeDtypeStruct((T, D), F32),
        scratch_shapes=[pltpu.VMEM((tm, D), BF16)],
        compiler_params=_params(("parallel", "arbitrary"), vmem),
        name="ffn",
    )(x, gain, wg, wu, wd)


def _proj_kernel(x_ref, g_ref, w_ref, qg_ref, kg_ref, ca_ref, sa_ref, cr_ref, sr_ref, o_ref, xn_ref):
    n = pl.program_id(1)

    @pl.when(n == 0)
    def _():
        xn_ref[...] = _rms_norm_rows(x_ref[...], g_ref[...]).astype(BF16)

    y = jnp.dot(xn_ref[...], w_ref[...], preferred_element_type=F32)

    def attn_qk(gain_ref):
        cos, sin = ca_ref[...], sa_ref[...]
        for h in range(ATTN_HEADS):
            cols = slice(h * ATTN_HEAD_DIM, (h + 1) * ATTN_HEAD_DIM)
            yn = _rms_norm_rows(y[:, cols], gain_ref[...])
            o_ref[:, cols] = yn * cos + pltpu.roll(yn, ATTN_HEAD_DIM // 2, 1) * sin

    def ret_qk(scale):
        cos, sin = cr_ref[...], sr_ref[...]
        half = RET_DIM // 2
        for h in range(RET_HEADS):
            lo = y[:, h * RET_DIM:h * RET_DIM + half]
            hi = y[:, h * RET_DIM + half:(h + 1) * RET_DIM]
            o_ref[:, h * RET_DIM:h * RET_DIM + half] = (lo * cos[:, :half] + hi * sin[:, :half]) * scale
            o_ref[:, h * RET_DIM + half:(h + 1) * RET_DIM] = (hi * cos[:, half:] + lo * sin[:, half:]) * scale

    @pl.when(n == 0)
    def _():
        attn_qk(qg_ref)

    @pl.when(n == 1)
    def _():
        attn_qk(kg_ref)

    @pl.when(n == 3)
    def _():
        ret_qk(1.0)

    @pl.when(n == 4)
    def _():
        ret_qk(RET_DIM ** -0.5)

    @pl.when((n == 2) | (n >= 5))
    def _():
        o_ref[...] = y


def _rope_tables(seq, head_dim):
    half = head_dim // 2
    inv_freq = ROPE_THETA ** (-jnp.arange(half, dtype=F32) / half)
    ang = jnp.arange(seq, dtype=F32)[:, None] * inv_freq[None, :]
    cos, sin = jnp.cos(ang), jnp.sin(ang)
    return jnp.concatenate([cos, cos], axis=-1), jnp.concatenate([-sin, sin], axis=-1)


def _proj(x, gain, w_in, q_gain, k_gain, tables, seq):
    T, D = x.shape
    tm = ROW_TILE
    nseq = seq // tm
    ca, sa, cr, sr = tables
    pos = lambda i, n: (i % nseq, 0)
    const = lambda i, n: (0, 0)
    vmem = 2 * tm * D * 4 + tm * D * 2 + 2 * D * SEG * 2 + 2 * tm * SEG * 4 + 2 * 2 * tm * (128 + 256) * 4 \
        + 3 * tm * SEG * 4 + 4 * MIB
    return pl.pallas_call(
        _proj_kernel,
        grid=(T // tm, IN_COLS // SEG),
        in_specs=[
            pl.BlockSpec((tm, D), lambda i, n: (i, 0)),
            pl.BlockSpec((1, D), const),
            pl.BlockSpec((D, SEG), lambda i, n: (0, n)),
            pl.BlockSpec((1, ATTN_HEAD_DIM), const),
            pl.BlockSpec((1, ATTN_HEAD_DIM), const),
            pl.BlockSpec((tm, ATTN_HEAD_DIM), pos),
            pl.BlockSpec((tm, ATTN_HEAD_DIM), pos),
            pl.BlockSpec((tm, RET_DIM), pos),
            pl.BlockSpec((tm, RET_DIM), pos),
        ],
        out_specs=pl.BlockSpec((tm, SEG), lambda i, n: (i, n)),
        out_shape=jax.ShapeDtypeStruct((T, IN_COLS), F32),
        scratch_shapes=[pltpu.VMEM((tm, D), BF16)],
        compiler_params=_params(("parallel", "arbitrary"), vmem),
        name="proj",
    )(x, gain, w_in, q_gain, k_gain, ca, sa, cr, sr)


def _attn_kernel(q_ref, k_ref, v_ref, o_ref, po_ref, pl_ref, *, seq):
    scale = ATTN_HEAD_DIM ** -0.5
    iq = lax.broadcasted_iota(jnp.int32, (ATTN_QBLK, ATTN_KWIN), 0)
    ik = lax.broadcasted_iota(jnp.int32, (ATTN_QBLK, ATTN_KWIN), 1)
    rel = iq - ik
    blocks_per_super = ATTN_SUPER // ATTN_QBLK

    def super_block(sb, carry):
        base = pl.multiple_of(sb * ATTN_SUPER, ATTN_SUPER)
        for p, d in enumerate(DILATIONS):
            sub_len = seq // d
            per_res = blocks_per_super // d

            def block(u, c, p=p, d=d, sub_len=sub_len, per_res=per_res):
                r = u % d
                jl = u // d
                jb = sb * per_res + jl
                q0 = jb * ATTN_QBLK
                k0 = jnp.clip(q0 - ATTN_HALF, 0, sub_len - ATTN_KWIN)
                if d == 1:
                    qs, ks = pl.ds(q0, ATTN_QBLK), pl.ds(k0, ATTN_KWIN)
                    os = pl.ds(jl * ATTN_QBLK, ATTN_QBLK)
                else:
                    qs = pl.ds(r + d * q0, ATTN_QBLK, stride=d)
                    ks = pl.ds(r + d * k0, ATTN_KWIN, stride=d)
                    os = pl.ds(r + d * jl * ATTN_QBLK, ATTN_QBLK, stride=d)
                q = q_ref[qs, :].astype(BF16)
                k = k_ref[ks, :].astype(BF16)
                v = v_ref[ks, :].astype(BF16)
                s = lax.dot_general(q, k, (((1,), (1,)), ((), ())), preferred_element_type=F32) * scale
                valid = jnp.abs(rel + (q0 - k0)) <= ATTN_HALF
                s = jnp.where(valid, s, NEG_INF)
                m = jnp.max(s, axis=-1, keepdims=True)
                e = jnp.exp(s - m)
                den = jnp.sum(e, axis=-1, keepdims=True)
                out = jnp.dot(e.astype(BF16), v, preferred_element_type=F32) / den
                po_ref[p, os, :] = out
                pl_ref[p, os, :] = jnp.broadcast_to(m + jnp.log(den), (ATTN_QBLK, ATTN_HEAD_DIM))
                return c

            lax.fori_loop(0, blocks_per_super, block, 0)

        def merge(i, c):
            rows = pl.ds(pl.multiple_of(i * ATTN_QBLK, ATTN_QBLK), ATTN_QBLK)
            lse = [pl_ref[p, rows, :] for p in range(len(DILATIONS))]
            top = functools.reduce(jnp.maximum, lse)
            w = [jnp.exp(l - top) for l in lse]
            tot = functools.reduce(jnp.add, w)
            acc = functools.reduce(jnp.add, [(w[p] / tot) * po_ref[p, rows, :] for p in range(len(DILATIONS))])
            o_ref[pl.ds(base + i * ATTN_QBLK, ATTN_QBLK), :] = acc.astype(o_ref.dtype)
            return c

        lax.fori_loop(0, blocks_per_super, merge, 0)
        return carry

    lax.fori_loop(0, seq // ATTN_SUPER, super_block, 0)


def _attn(p3):
    B, S, _ = p3.shape
    assert S % ATTN_SUPER == 0 and S // DILATIONS[-1] >= ATTN_KWIN
    blk = (None, S, ATTN_HEAD_DIM)
    vmem = 2 * 3 * S * 128 * 4 + 2 * S * 128 * 2 + 2 * len(DILATIONS) * ATTN_SUPER * 128 * 4 + 8 * MIB
    return pl.pallas_call(
        functools.partial(_attn_kernel, seq=S),
        grid=(B, ATTN_HEADS),
        in_specs=[
            pl.BlockSpec(blk, lambda b, h: (b, 0, h)),
            pl.BlockSpec(blk, lambda b, h: (b, 0, ATTN_HEADS + h)),
            pl.BlockSpec(blk, lambda b, h: (b, 0, 2 * ATTN_HEADS + h)),
        ],
        out_specs=pl.BlockSpec(blk, lambda b, h: (b, 0, h)),
        out_shape=jax.ShapeDtypeStruct((B, S, ATTN_WIDTH), BF16),
        scratch_shapes=[pltpu.VMEM((len(DILATIONS), ATTN_SUPER, ATTN_HEAD_DIM), F32),
                        pltpu.VMEM((len(DILATIONS), ATTN_SUPER, ATTN_HEAD_DIM), F32)],
        compiler_params=_params(("parallel", "parallel"), vmem),
        name="attn",
    )(p3, p3, p3)


def _decay_tables(lg, backward):
    C = RET_CHUNK
    i = lax.broadcasted_iota(jnp.int32, (C, C), 0)
    j = lax.broadcasted_iota(jnp.int32, (C, C), 1)
    diff = (j - i) if backward else (i - j)
    mask = jnp.where(diff >= 0, jnp.exp(lg * jnp.maximum(diff, 0).astype(F32)), 0.0)
    idx = lax.broadcasted_iota(jnp.int32, (C, 1), 0).astype(F32)
    if backward:
        k_pow, q_pow = idx, C - idx
    else:
        k_pow, q_pow = C - 1 - idx, idx + 1
    chunk = jnp.exp(jnp.full((1, RET_DIM), C, F32) * lg)
    return mask, jnp.exp(lg * k_pow), jnp.exp(lg * q_pow), chunk


def _ret_chunk(q, k, v, state, tables):
    mask, k_scale, q_scale, chunk_decay = tables
    vb = v.astype(BF16)
    s = lax.dot_general(q.astype(BF16), k.astype(BF16), (((1,), (1,)), ((), ())), preferred_element_type=F32)
    inner = jnp.dot((s * mask).astype(BF16), vb, preferred_element_type=F32)
    cross = jnp.dot((q * q_scale).astype(BF16), state.astype(BF16), preferred_element_type=F32)
    kv = lax.dot_general((k * k_scale).astype(BF16), vb, (((0,), (0,)), ((), ())), preferred_element_type=F32)
    return inner + cross, chunk_decay * state + kv


def _ret_fwd_kernel(ld_ref, q_ref, k_ref, v_ref, o_ref, state_ref):
    h, t = pl.program_id(1), pl.program_id(2)

    @pl.when(t == 0)
    def _():
        state_ref[...] = jnp.zeros_like(state_ref)

    tables = _decay_tables(ld_ref[h], backward=False)
    state = state_ref[...]
    for c in range(RET_ROWS // RET_CHUNK):
        rows = slice(c * RET_CHUNK, (c + 1) * RET_CHUNK)
        out, state = _ret_chunk(q_ref[rows, :], k_ref[rows, :], v_ref[rows, :], state, tables)
        o_ref[rows, :] = out
    state_ref[...] = state


def _ret_bwd_kernel(ld_ref, q_ref, k_ref, v_ref, gate_ref, fwd_ref, gain_ref, o_ref, state_ref):
    h, t = pl.program_id(1), pl.program_id(2)

    @pl.when(t == 0)
    def _():
        state_ref[...] = jnp.zeros_like(state_ref)

    tables = _decay_tables(ld_ref[h], backward=True)
    state = state_ref[...]
    for c in reversed(range(RET_ROWS // RET_CHUNK)):
        rows = slice(c * RET_CHUNK, (c + 1) * RET_CHUNK)
        out, state = _ret_chunk(q_ref[rows, :], k_ref[rows, :], v_ref[rows, :], state, tables)
        both = _rms_norm_rows(fwd_ref[rows, :] + out, gain_ref[...])
        o_ref[rows, :] = (jax.nn.silu(gate_ref[rows, :]) * both).astype(o_ref.dtype)
    state_ref[...] = state


def _ret_specs(n_t, reverse):
    tmap = (lambda t: n_t - 1 - t) if reverse else (lambda t: t)
    blk = (None, RET_ROWS, RET_DIM)
    seg0 = 3 * ATTN_WIDTH // RET_DIM

    def col(seg):
        return pl.BlockSpec(blk, lambda b, h, t: (b, tmap(t), seg0 + seg * RET_HEADS + h))

    return blk, tmap, col


def _ret_fwd(p3, log_decay):
    B, S, _ = p3.shape
    n_t = S // RET_ROWS
    blk, tmap, col = _ret_specs(n_t, reverse=False)
    vmem = 2 * 4 * RET_ROWS * RET_DIM * 4 + RET_DIM * RET_DIM * 4 + 8 * MIB
    return pl.pallas_call(
        _ret_fwd_kernel,
        grid=(B, RET_HEADS, n_t),
        in_specs=[pl.BlockSpec(memory_space=pltpu.SMEM), col(0), col(1), col(2)],
        out_specs=pl.BlockSpec(blk, lambda b, h, t: (b, t, h)),
        out_shape=jax.ShapeDtypeStruct((B, S, RET_WIDTH), F32),
        scratch_shapes=[pltpu.VMEM((RET_DIM, RET_DIM), F32)],
        compiler_params=_params(("parallel", "parallel", "arbitrary"), vmem),
        name="ret_fwd",
    )(log_decay, p3, p3, p3)


def _ret_bwd(p3, fwd, log_decay, out_gain):
    B, S, _ = p3.shape
    n_t = S // RET_ROWS
    blk, tmap, col = _ret_specs(n_t, reverse=True)
    vmem = 2 * 6 * RET_ROWS * RET_DIM * 4 + RET_DIM * RET_DIM * 4 + 8 * MIB
    return pl.pallas_call(
        _ret_bwd_kernel,
        grid=(B, RET_HEADS, n_t),
        in_specs=[pl.BlockSpec(memory_space=pltpu.SMEM), col(0), col(1), col(2), col(3),
                  pl.BlockSpec(blk, lambda b, h, t: (b, tmap(t), h)),
                  pl.BlockSpec((None, 1, RET_DIM), lambda b, h, t: (h, 0, 0))],
        out_specs=pl.BlockSpec(blk, lambda b, h, t: (b, tmap(t), h)),
        out_shape=jax.ShapeDtypeStruct((B, S, RET_WIDTH), BF16),
        scratch_shapes=[pltpu.VMEM((RET_DIM, RET_DIM), F32)],
        compiler_params=_params(("parallel", "parallel", "arbitrary"), vmem),
        name="ret_bwd",
    )(log_decay, p3, p3, p3, p3, fwd, out_gain)


def _mix_out_kernel(x_ref, a_ref, r_ref, w_ref, o_ref):
    acc = jnp.dot(a_ref[...], w_ref[:ATTN_WIDTH, :], preferred_element_type=F32)
    acc += jnp.dot(r_ref[...], w_ref[ATTN_WIDTH:, :], preferred_element_type=F32)
    o_ref[...] = x_ref[...] + acc


def _mix_out(x, a, r, w_out):
    T, D = x.shape
    tm = ROW_TILE
    vmem = 2 * (2 * tm * D * 4 + 2 * tm * SEG * 2 + D * D * 2) + tm * D * 4 + 4 * MIB
    return pl.pallas_call(
        _mix_out_kernel,
        grid=(T // tm,),
        in_specs=[
            pl.BlockSpec((tm, D), lambda i: (i, 0)),
            pl.BlockSpec((tm, ATTN_WIDTH), lambda i: (i, 0)),
            pl.BlockSpec((tm, RET_WIDTH), lambda i: (i, 0)),
            pl.BlockSpec((D, D), lambda i: (0, 0)),
        ],
        out_specs=pl.BlockSpec((tm, D), lambda i: (i, 0)),
        out_shape=jax.ShapeDtypeStruct((T, D), F32),
        compiler_params=_params(("parallel",), vmem),
        name="mix_out",
    )(x, a, r, w_out)


def _pad_ff(w, axis):
    pad = [(0, 0)] * w.ndim
    pad[axis] = (0, D_FF_PAD - D_FF)
    return jnp.pad(w.astype(BF16), pad)


def _trunk(x3, layers):
    B, S, D = x3.shape
    T = B * S
    x = x3.reshape(T, D)
    tables = _rope_tables(S, ATTN_HEAD_DIM) + _rope_tables(S, RET_DIM)
    for lw in layers:
        x = _ffn(x, lw["ffn1_norm"], lw["ffn1_wg"], lw["ffn1_wu"], lw["ffn1_wd"])
        proj = _proj(x, lw["mix_norm"], lw["w_in"], lw["q_gain"], lw["k_gain"], tables, S)
        p3 = proj.reshape(B, S, IN_COLS)
        a = _attn(p3)
        fwd = _ret_fwd(p3, lw["ld_fwd"])
        r = _ret_bwd(p3, fwd, lw["ld_bwd"], lw["ret_gain"])
        x = _mix_out(x, a.reshape(T, ATTN_WIDTH), r.reshape(T, RET_WIDTH), lw["w_out"])
        x = _ffn(x, lw["ffn2_norm"], lw["ffn2_wg"], lw["ffn2_wu"], lw["ffn2_wd"])
    return x.reshape(B, S, D)


def kernel(x_prompt, x_sample, ffn1_norm, ffn1_w_gate, ffn1_w_up, ffn1_w_down, mix_norm, w_in, attn_q_norm, attn_k_norm, ret_log_decay_fwd, ret_log_decay_bwd, ret_out_norm, w_out, ffn2_norm, ffn2_w_gate, ffn2_w_up, ffn2_w_down):
    depth = w_in.shape[0]
    layers = []
    for l in range(depth):
        layers.append({
            "ffn1_norm": ffn1_norm[l][None, :].astype(F32),
            "ffn1_wg": _pad_ff(ffn1_w_gate[l], 1),
            "ffn1_wu": _pad_ff(ffn1_w_up[l], 1),
            "ffn1_wd": _pad_ff(ffn1_w_down[l], 0),
            "mix_norm": mix_norm[l][None, :].astype(F32),
            "w_in": w_in[l].astype(BF16),
            "q_gain": attn_q_norm[l][None, :].astype(F32),
            "k_gain": attn_k_norm[l][None, :].astype(F32),
            "ld_fwd": ret_log_decay_fwd[l].astype(F32),
            "ld_bwd": ret_log_decay_bwd[l].astype(F32),
            "ret_gain": ret_out_norm[l][:, None, :].astype(F32),
            "w_out": w_out[l].astype(BF16),
            "ffn2_norm": ffn2_norm[l][None, :].astype(F32),
            "ffn2_wg": _pad_ff(ffn2_w_gate[l], 1),
            "ffn2_wu": _pad_ff(ffn2_w_up[l], 1),
            "ffn2_wd": _pad_ff(ffn2_w_down[l], 0),
        })
    return (_trunk(x_prompt, layers), _trunk(x_sample, layers))
```

```python
import functools

import jax
import jax.numpy as jnp
from jax import lax
from jax.experimental import pallas as pl
from jax.experimental.pallas import tpu as pltpu

F32 = jnp.float32
BF16 = jnp.bfloat16

D_MODEL = 2048
ATTN_HEAD_DIM = 128
ATTN_HEADS = 8
ATTN_WIDTH = ATTN_HEADS * ATTN_HEAD_DIM
DILATIONS = (1, 4, 16)
ATTN_HALF = 64
RET_HEADS = 4
RET_DIM = 256
RET_WIDTH = RET_HEADS * RET_DIM
RET_CHUNK = 128
D_FF = 5504
IN_COLS = 3 * ATTN_WIDTH + 4 * RET_WIDTH
ROPE_THETA = 10000.0
NORM_EPS = 1e-6
NEG_INF = -1e30
LOG2_E = 1.4426950408889634

V7X_LANES = 128
FF_TILE = 512
D_FF_PAD = -(-D_FF // FF_TILE) * FF_TILE
ROW_TILE = 512
CAST_TILE_ELEMS = 1024 * 1024
SEG = 1024
PROJ_CHUNK = 256
ATTN_QBLK = 128
ATTN_KWIN = 256
ATTN_SUPER = ATTN_QBLK * DILATIONS[-1]
ATTN_UNROLL = 16
RET_ROWS = 512
MIB = 1024 * 1024


def _params(semantics, vmem_bytes):
    return pltpu.CompilerParams(dimension_semantics=semantics, vmem_limit_bytes=int(vmem_bytes))


def _rms_norm_rows(x, gain):
    return x * lax.rsqrt(jnp.mean(x * x, axis=-1, keepdims=True) + NORM_EPS) * gain


def _cast_kernel(w_ref, o_ref, *, rows, cols):
    tr, tc = o_ref.shape
    w = w_ref[...]
    if rows % tr or cols % tc:
        r = lax.broadcasted_iota(jnp.int32, (tr, tc), 0) + pl.program_id(0) * tr
        c = lax.broadcasted_iota(jnp.int32, (tr, tc), 1) + pl.program_id(1) * tc
        w = jnp.where((r < rows) & (c < cols), w, 0.0)
    o_ref[...] = w.astype(o_ref.dtype)


def _cast_layer(w, layer, rows_pad, cols_pad):
    _, rows, cols = w.shape
    tc = next(c for c in (2048, 1024, 512) if cols_pad % c == 0)
    tr = CAST_TILE_ELEMS // tc
    assert rows_pad % tr == 0
    return pl.pallas_call(
        functools.partial(_cast_kernel, rows=rows, cols=cols),
        grid=(rows_pad // tr, cols_pad // tc),
        in_specs=[pl.BlockSpec((None, tr, tc), lambda i, j: (layer, i, j))],
        out_specs=pl.BlockSpec((tr, tc), lambda i, j: (i, j)),
        out_shape=jax.ShapeDtypeStruct((rows_pad, cols_pad), BF16),
        compiler_params=_params(("parallel", "parallel"), 2 * tr * tc * (4 + 2) + 4 * MIB),
        name="cast",
    )(w)


def _ffn_kernel(x_ref, g_ref, wg_ref, wu_ref, wd_ref, o_ref, xn_ref):
    f = pl.program_id(1)

    @pl.when(f == 0)
    def _():
        xn_ref[...] = _rms_norm_rows(x_ref[...], g_ref[...]).astype(BF16)
        o_ref[...] = jnp.zeros_like(o_ref)

    xn = xn_ref[...]
    gate = jnp.dot(xn, wg_ref[...], preferred_element_type=F32)
    up = jnp.dot(xn, wu_ref[...], preferred_element_type=F32)
    h = (jax.nn.silu(gate) * up).astype(BF16)
    o_ref[...] += jnp.dot(h, wd_ref[...], preferred_element_type=F32)

    @pl.when(f == pl.num_programs(1) - 1)
    def _():
        o_ref[...] = x_ref[...] + 0.5 * o_ref[...]


def _ffn(x, gain, wg, wu, wd):
    T, D = x.shape
    tm, tf = ROW_TILE, FF_TILE
    vmem = 2 * (2 * tm * D * 4) + tm * D * 2 + 2 * 3 * D * tf * 2 + 4 * tm * tf * 4 + 4 * MIB
    return pl.pallas_call(
        _ffn_kernel,
        grid=(T // tm, D_FF_PAD // tf),
        in_specs=[
            pl.BlockSpec((tm, D), lambda i, f: (i, 0)),
            pl.BlockSpec((1, D), lambda i, f: (0, 0)),
            pl.BlockSpec((D, tf), lambda i, f: (0, f)),
            pl.BlockSpec((D, tf), lambda i, f: (0, f)),
            pl.BlockSpec((tf, D), lambda i, f: (f, 0)),
        ],
        out_specs=pl.BlockSpec((tm, D), lambda i, f: (i, 0)),
        out_shape=jax.ShapeDtypeStruct((T, D), F32),
        scratch_shapes=[pltpu.VMEM((tm, D), BF16)],
        compiler_params=_params(("parallel", "arbitrary"), vmem),
        name="ffn",
    )(x, gain, wg, wu, wd)


def _proj_kernel(x_ref, g_ref, w_ref, qg_ref, kg_ref, ca_ref, sa_ref, cr_ref, sr_ref, o_ref, xn_ref):
    n = pl.program_id(1)

    @pl.when(n == 0)
    def _():
        xn_ref[...] = _rms_norm_rows(x_ref[...], g_ref[...]).astype(BF16)

    def by_chunks(epilogue):
        xn = xn_ref[...]
        for j in range(SEG // PROJ_CHUNK):
            c0 = j * PROJ_CHUNK
            y = jnp.dot(xn, w_ref[:, c0:c0 + PROJ_CHUNK], preferred_element_type=F32)
            epilogue(y, c0)

    def attn_qk(gain_ref):
        def epilogue(y, c0):
            cos, sin = ca_ref[...], sa_ref[...]
            for h0 in range(0, PROJ_CHUNK, ATTN_HEAD_DIM):
                yn = _rms_norm_rows(y[:, h0:h0 + ATTN_HEAD_DIM], gain_ref[...])
                o_ref[:, c0 + h0:c0 + h0 + ATTN_HEAD_DIM] = yn * cos + pltpu.roll(yn, ATTN_HEAD_DIM // 2, 1) * sin
        by_chunks(epilogue)

    def ret_qk(scale):
        def epilogue(y, c0):
            cos, sin = cr_ref[...], sr_ref[...]
            half = RET_DIM // 2
            lo, hi = y[:, :half], y[:, half:]
            o_ref[:, c0:c0 + half] = (lo * cos[:, :half] + hi * sin[:, :half]) * scale
            o_ref[:, c0 + half:c0 + RET_DIM] = (hi * cos[:, half:] + lo * sin[:, half:]) * scale
        by_chunks(epilogue)

    def plain():
        def epilogue(y, c0):
            o_ref[:, c0:c0 + PROJ_CHUNK] = y
        by_chunks(epilogue)

    @pl.when(n == 0)
    def _():
        attn_qk(qg_ref)

    @pl.when(n == 1)
    def _():
        attn_qk(kg_ref)

    @pl.when(n == 3)
    def _():
        ret_qk(1.0)

    @pl.when(n == 4)
    def _():
        ret_qk(RET_DIM ** -0.5)

    @pl.when((n == 2) | (n >= 5))
    def _():
        plain()


def _rope_tables(seq, head_dim):
    half = head_dim // 2
    inv_freq = ROPE_THETA ** (-jnp.arange(half, dtype=F32) / half)
    ang = jnp.arange(seq, dtype=F32)[:, None] * inv_freq[None, :]
    cos, sin = jnp.cos(ang), jnp.sin(ang)
    return jnp.concatenate([cos, cos], axis=-1), jnp.concatenate([-sin, sin], axis=-1)


def _proj(x, gain, w_in, q_gain, k_gain, tables, seq):
    T, D = x.shape
    tm = ROW_TILE
    nseq = seq // tm
    ca, sa, cr, sr = tables
    pos = lambda i, n: (i % nseq, 0)
    const = lambda i, n: (0, 0)
    vmem = 2 * tm * D * 4 + tm * D * 2 + 2 * D * SEG * 2 + 2 * tm * SEG * 4 + 2 * 2 * tm * (128 + 256) * 4 \
        + 3 * tm * SEG * 4 + 4 * MIB
    return pl.pallas_call(
        _proj_kernel,
        grid=(T // tm, IN_COLS // SEG),
        in_specs=[
            pl.BlockSpec((tm, D), lambda i, n: (i, 0)),
            pl.BlockSpec((1, D), const),
            pl.BlockSpec((D, SEG), lambda i, n: (0, n)),
            pl.BlockSpec((1, ATTN_HEAD_DIM), const),
            pl.BlockSpec((1, ATTN_HEAD_DIM), const),
            pl.BlockSpec((tm, ATTN_HEAD_DIM), pos),
            pl.BlockSpec((tm, ATTN_HEAD_DIM), pos),
            pl.BlockSpec((tm, RET_DIM), pos),
            pl.BlockSpec((tm, RET_DIM), pos),
        ],
        out_specs=pl.BlockSpec((tm, SEG), lambda i, n: (i, n)),
        out_shape=jax.ShapeDtypeStruct((T, IN_COLS), F32),
        scratch_shapes=[pltpu.VMEM((tm, D), BF16)],
        compiler_params=_params(("parallel", "arbitrary"), vmem),
        name="proj",
    )(x, gain, w_in, q_gain, k_gain, ca, sa, cr, sr)


def _attn_kernel(q_ref, k_ref, v_ref, o_ref, po_ref, pl_ref, bias_ref, *, seq):
    scale = ATTN_HEAD_DIM ** -0.5
    exp2_scale = scale * LOG2_E
    blocks_per_super = ATTN_SUPER // ATTN_QBLK

    rel = (lax.broadcasted_iota(jnp.int32, (ATTN_QBLK, ATTN_KWIN), 0)
           - lax.broadcasted_iota(jnp.int32, (ATTN_QBLK, ATTN_KWIN), 1))
    for case in range(3):
        bias_ref[case] = jnp.where(jnp.abs(rel + case * ATTN_HALF) <= ATTN_HALF, 0.0, NEG_INF)

    def super_block(sb, carry):
        base = pl.multiple_of(sb * ATTN_SUPER, ATTN_SUPER)
        for p, d in enumerate(DILATIONS):
            sub_len = seq // d
            per_res = blocks_per_super // d

            def block(u, c, p=p, d=d, sub_len=sub_len, per_res=per_res):
                r = u % d
                jl = u // d
                jb = sb * per_res + jl
                q0 = jb * ATTN_QBLK
                k0 = jnp.clip(q0 - ATTN_HALF, 0, sub_len - ATTN_KWIN)
                if d == 1:
                    qs, ks = pl.ds(q0, ATTN_QBLK), pl.ds(k0, ATTN_KWIN)
                    os = pl.ds(jl * ATTN_QBLK, ATTN_QBLK)
                else:
                    qs = pl.ds(r + d * q0, ATTN_QBLK, stride=d)
                    ks = pl.ds(r + d * k0, ATTN_KWIN, stride=d)
                    os = pl.ds(r + d * jl * ATTN_QBLK, ATTN_QBLK, stride=d)
                q = q_ref[qs, :].astype(BF16)
                k = k_ref[ks, :].astype(BF16)
                v = v_ref[ks, :].astype(BF16)
                s = lax.dot_general(q, k, (((1,), (1,)), ((), ())), preferred_element_type=F32)
                s = s + bias_ref[(q0 - k0) // ATTN_HALF]
                m = jnp.max(s, axis=-1, keepdims=True)
                e = jnp.exp2((s - m) * exp2_scale)
                den = jnp.sum(e, axis=-1, keepdims=True)
                out = jnp.dot(e.astype(BF16), v, preferred_element_type=F32) / den
                po_ref[p, os, :] = out
                pl_ref[p, os, :] = jnp.broadcast_to(m * scale + jnp.log(den), (ATTN_QBLK, ATTN_HEAD_DIM))
                return c

            lax.fori_loop(0, blocks_per_super, block, 0, unroll=ATTN_UNROLL)

        def merge(i, c):
            rows = pl.ds(pl.multiple_of(i * ATTN_QBLK, ATTN_QBLK), ATTN_QBLK)
            lse = [pl_ref[p, rows, :] for p in range(len(DILATIONS))]
            top = functools.reduce(jnp.maximum, lse)
            w = [jnp.exp(l - top) for l in lse]
            tot = functools.reduce(jnp.add, w)
            acc = functools.reduce(jnp.add, [(w[p] / tot) * po_ref[p, rows, :] for p in range(len(DILATIONS))])
            o_ref[pl.ds(base + i * ATTN_QBLK, ATTN_QBLK), :] = acc.astype(o_ref.dtype)
            return c

        lax.fori_loop(0, blocks_per_super, merge, 0)
        return carry

    lax.fori_loop(0, seq // ATTN_SUPER, super_block, 0)


def _attn(p3):
    B, S, _ = p3.shape
    assert S % ATTN_SUPER == 0 and S // DILATIONS[-1] >= ATTN_KWIN
    blk = (None, S, ATTN_HEAD_DIM)
    vmem = 2 * 3 * S * 128 * 4 + 2 * S * 128 * 2 + 2 * len(DILATIONS) * ATTN_SUPER * 128 * 4 + 8 * MIB
    return pl.pallas_call(
        functools.partial(_attn_kernel, seq=S),
        grid=(B, ATTN_HEADS),
        in_specs=[
            pl.BlockSpec(blk, lambda b, h: (b, 0, h)),
            pl.BlockSpec(blk, lambda b, h: (b, 0, ATTN_HEADS + h)),
            pl.BlockSpec(blk, lambda b, h: (b, 0, 2 * ATTN_HEADS + h)),
        ],
        out_specs=pl.BlockSpec(blk, lambda b, h: (b, 0, h)),
        out_shape=jax.ShapeDtypeStruct((B, S, ATTN_WIDTH), BF16),
        scratch_shapes=[pltpu.VMEM((len(DILATIONS), ATTN_SUPER, ATTN_HEAD_DIM), F32),
                        pltpu.VMEM((len(DILATIONS), ATTN_SUPER, ATTN_HEAD_DIM), F32),
                        pltpu.VMEM((3, ATTN_QBLK, ATTN_KWIN), F32)],
        compiler_params=_params(("parallel", "parallel"), vmem),
        name="attn",
    )(p3, p3, p3)


def _decay_tables(lg, backward):
    C = RET_CHUNK
    i = lax.broadcasted_iota(jnp.int32, (C, C), 0)
    j = lax.broadcasted_iota(jnp.int32, (C, C), 1)
    diff = (j - i) if backward else (i - j)
    mask = jnp.where(diff >= 0, jnp.exp(lg * jnp.maximum(diff, 0).astype(F32)), 0.0)
    idx = lax.broadcasted_iota(jnp.int32, (C, 1), 0).astype(F32)
    if backward:
        k_pow, q_pow = idx, C - idx
    else:
        k_pow, q_pow = C - 1 - idx, idx + 1
    chunk = jnp.exp(jnp.full((1, RET_DIM), C, F32) * lg)
    return mask, jnp.exp(lg * k_pow), jnp.exp(lg * q_pow), chunk


def _ret_chunk(q, k, v, state, tables):
    mask, k_scale, q_scale, chunk_decay = tables
    vb = v.astype(BF16)
    s = lax.dot_general(q.astype(BF16), k.astype(BF16), (((1,), (1,)), ((), ())), preferred_element_type=F32)
    inner = jnp.dot((s * mask).astype(BF16), vb, preferred_element_type=F32)
    cross = jnp.dot((q * q_scale).astype(BF16), state.astype(BF16), preferred_element_type=F32)
    kv = lax.dot_general((k * k_scale).astype(BF16), vb, (((0,), (0,)), ((), ())), preferred_element_type=F32)
    return inner + cross, chunk_decay * state + kv


def _ret_fwd_kernel(ld_ref, q_ref, k_ref, v_ref, o_ref, state_ref):
    h, t = pl.program_id(1), pl.program_id(2)

    @pl.when(t == 0)
    def _():
        state_ref[...] = jnp.zeros_like(state_ref)

    tables = _decay_tables(ld_ref[h], backward=False)
    state = state_ref[...]
    for c in range(RET_ROWS // RET_CHUNK):
        rows = slice(c * RET_CHUNK, (c + 1) * RET_CHUNK)
        out, state = _ret_chunk(q_ref[rows, :], k_ref[rows, :], v_ref[rows, :], state, tables)
        o_ref[rows, :] = out
    state_ref[...] = state


def _ret_bwd_kernel(ld_ref, q_ref, k_ref, v_ref, gate_ref, fwd_ref, gain_ref, o_ref, state_ref):
    h, t = pl.program_id(1), pl.program_id(2)

    @pl.when(t == 0)
    def _():
        state_ref[...] = jnp.zeros_like(state_ref)

    tables = _decay_tables(ld_ref[h], backward=True)
    state = state_ref[...]
    for c in reversed(range(RET_ROWS // RET_CHUNK)):
        rows = slice(c * RET_CHUNK, (c + 1) * RET_CHUNK)
        out, state = _ret_chunk(q_ref[rows, :], k_ref[rows, :], v_ref[rows, :], state, tables)
        both = _rms_norm_rows(fwd_ref[rows, :] + out, gain_ref[...])
        o_ref[rows, :] = (jax.nn.silu(gate_ref[rows, :]) * both).astype(o_ref.dtype)
    state_ref[...] = state


def _ret_specs(n_t, reverse):
    tmap = (lambda t: n_t - 1 - t) if reverse else (lambda t: t)
    blk = (None, RET_ROWS, RET_DIM)
    seg0 = 3 * ATTN_WIDTH // RET_DIM

    def col(seg):
        return pl.BlockSpec(blk, lambda b, h, t: (b, tmap(t), seg0 + seg * RET_HEADS + h))

    return blk, tmap, col


def _ret_fwd(p3, log_decay):
    B, S, _ = p3.shape
    n_t = S // RET_ROWS
    blk, tmap, col = _ret_specs(n_t, reverse=False)
    vmem = 2 * 4 * RET_ROWS * RET_DIM * 4 + RET_DIM * RET_DIM * 4 + 8 * MIB
    return pl.pallas_call(
        _ret_fwd_kernel,
        grid=(B, RET_HEADS, n_t),
        in_specs=[pl.BlockSpec(memory_space=pltpu.SMEM), col(0), col(1), col(2)],
        out_specs=pl.BlockSpec(blk, lambda b, h, t: (b, t, h)),
        out_shape=jax.ShapeDtypeStruct((B, S, RET_WIDTH), F32),
        scratch_shapes=[pltpu.VMEM((RET_DIM, RET_DIM), F32)],
        compiler_params=_params(("parallel", "parallel", "arbitrary"), vmem),
        name="ret_fwd",
    )(log_decay, p3, p3, p3)


def _ret_bwd(p3, fwd, log_decay, out_gain):
    B, S, _ = p3.shape
    n_t = S // RET_ROWS
    blk, tmap, col = _ret_specs(n_t, reverse=True)
    vmem = 2 * 6 * RET_ROWS * RET_DIM * 4 + RET_DIM * RET_DIM * 4 + 8 * MIB
    return pl.pallas_call(
        _ret_bwd_kernel,
        grid=(B, RET_HEADS, n_t),
        in_specs=[pl.BlockSpec(memory_space=pltpu.SMEM), col(0), col(1), col(2), col(3),
                  pl.BlockSpec(blk, lambda b, h, t: (b, tmap(t), h)),
                  pl.BlockSpec((None, 1, RET_DIM), lambda b, h, t: (h, 0, 0))],
        out_specs=pl.BlockSpec(blk, lambda b, h, t: (b, tmap(t), h)),
        out_shape=jax.ShapeDtypeStruct((B, S, RET_WIDTH), BF16),
        scratch_shapes=[pltpu.VMEM((RET_DIM, RET_DIM), F32)],
        compiler_params=_params(("parallel", "parallel", "arbitrary"), vmem),
        name="ret_bwd",
    )(log_decay, p3, p3, p3, p3, fwd, out_gain)


def _mix_out_kernel(x_ref, a_ref, r_ref, w_ref, o_ref):
    acc = jnp.dot(a_ref[...], w_ref[:ATTN_WIDTH, :], preferred_element_type=F32)
    acc += jnp.dot(r_ref[...], w_ref[ATTN_WIDTH:, :], preferred_element_type=F32)
    o_ref[...] = x_ref[...] + acc


def _mix_out(x, a, r, w_out):
    T, D = x.shape
    tm = ROW_TILE
    vmem = 2 * (2 * tm * D * 4 + 2 * tm * SEG * 2 + D * D * 2) + tm * D * 4 + 4 * MIB
    return pl.pallas_call(
        _mix_out_kernel,
        grid=(T // tm,),
        in_specs=[
            pl.BlockSpec((tm, D), lambda i: (i, 0)),
            pl.BlockSpec((tm, ATTN_WIDTH), lambda i: (i, 0)),
            pl.BlockSpec((tm, RET_WIDTH), lambda i: (i, 0)),
            pl.BlockSpec((D, D), lambda i: (0, 0)),
        ],
        out_specs=pl.BlockSpec((tm, D), lambda i: (i, 0)),
        out_shape=jax.ShapeDtypeStruct((T, D), F32),
        compiler_params=_params(("parallel",), vmem),
        name="mix_out",
    )(x, a, r, w_out)


def _trunk(x3, layers):
    B, S, D = x3.shape
    T = B * S
    x = x3.reshape(T, D)
    tables = _rope_tables(S, ATTN_HEAD_DIM) + _rope_tables(S, RET_DIM)
    for lw in layers:
        x = _ffn(x, lw["ffn1_norm"], lw["ffn1_wg"], lw["ffn1_wu"], lw["ffn1_wd"])
        proj = _proj(x, lw["mix_norm"], lw["w_in"], lw["q_gain"], lw["k_gain"], tables, S)
        p3 = proj.reshape(B, S, IN_COLS)
        a = _attn(p3)
        fwd = _ret_fwd(p3, lw["ld_fwd"])
        r = _ret_bwd(p3, fwd, lw["ld_bwd"], lw["ret_gain"])
        x = _mix_out(x, a.reshape(T, ATTN_WIDTH), r.reshape(T, RET_WIDTH), lw["w_out"])
        x = _ffn(x, lw["ffn2_norm"], lw["ffn2_wg"], lw["ffn2_wu"], lw["ffn2_wd"])
    return x.reshape(B, S, D)


def kernel(x_prompt, x_sample, ffn1_norm, ffn1_w_gate, ffn1_w_up, ffn1_w_down, mix_norm, w_in, attn_q_norm, attn_k_norm, ret_log_decay_fwd, ret_log_decay_bwd, ret_out_norm, w_out, ffn2_norm, ffn2_w_gate, ffn2_w_up, ffn2_w_down):
    depth = w_in.shape[0]
    layers = []
    for l in range(depth):
        layers.append({
            "ffn1_norm": ffn1_norm[l][None, :].astype(F32),
            "ffn1_wg": _cast_layer(ffn1_w_gate, l, D_MODEL, D_FF_PAD),
            "ffn1_wu": _cast_layer(ffn1_w_up, l, D_MODEL, D_FF_PAD),
            "ffn1_wd": _cast_layer(ffn1_w_down, l, D_FF_PAD, D_MODEL),
            "mix_norm": mix_norm[l][None, :].astype(F32),
            "w_in": _cast_layer(w_in, l, D_MODEL, IN_COLS),
            "q_gain": attn_q_norm[l][None, :].astype(F32),
            "k_gain": attn_k_norm[l][None, :].astype(F32),
            "ld_fwd": ret_log_decay_fwd[l].astype(F32),
            "ld_bwd": ret_log_decay_bwd[l].astype(F32),
            "ret_gain": ret_out_norm[l][:, None, :].astype(F32),
            "w_out": _cast_layer(w_out, l, D_MODEL, D_MODEL),
            "ffn2_norm": ffn2_norm[l][None, :].astype(F32),
            "ffn2_wg": _cast_layer(ffn2_w_gate, l, D_MODEL, D_FF_PAD),
            "ffn2_wu": _cast_layer(ffn2_w_up, l, D_MODEL, D_FF_PAD),
            "ffn2_wd": _cast_layer(ffn2_w_down, l, D_FF_PAD, D_MODEL),
        })
    return (_trunk(x_prompt, layers), _trunk(x_sample, layers))
```

```python
import functools

import jax
import jax.numpy as jnp
from jax import lax
from jax.experimental import pallas as pl
from jax.experimental.pallas import tpu as pltpu

F32 = jnp.float32
BF16 = jnp.bfloat16

D_MODEL = 2048
ATTN_HEAD_DIM = 128
ATTN_HEADS = 8
ATTN_WIDTH = ATTN_HEADS * ATTN_HEAD_DIM
DILATIONS = (1, 4, 16)
ATTN_HALF = 64
RET_HEADS = 4
RET_DIM = 256
RET_WIDTH = RET_HEADS * RET_DIM
RET_CHUNK = 128
D_FF = 5504
IN_COLS = 3 * ATTN_WIDTH + 4 * RET_WIDTH
ROPE_THETA = 10000.0
NORM_EPS = 1e-6
NEG_INF = -1e30
LOG2_E = 1.4426950408889634

V7X_LANES = 128
FF_TILE = 512
D_FF_PAD = -(-D_FF // FF_TILE) * FF_TILE
ROW_TILE = 512
FFN_ROWS = 1024
PROJ_ROWS = 1024
CAST_TILE_ELEMS = 1024 * 1024
SEG = 1024
PROJ_CHUNK = 256
ATTN_QBLK = 128
ATTN_KWIN = 256
ATTN_SUPER = ATTN_QBLK * DILATIONS[-1]
ATTN_UNROLL = 16
RET_ROWS = 1024
MIB = 1024 * 1024


def _params(semantics, vmem_bytes):
    return pltpu.CompilerParams(dimension_semantics=semantics, vmem_limit_bytes=int(vmem_bytes))


def _rms_norm_rows(x, gain):
    return x * lax.rsqrt(jnp.mean(x * x, axis=-1, keepdims=True) + NORM_EPS) * gain


def _cast_kernel(w_ref, o_ref, *, rows, cols):
    tr, tc = o_ref.shape
    w = w_ref[...]
    if rows % tr or cols % tc:
        r = lax.broadcasted_iota(jnp.int32, (tr, tc), 0) + pl.program_id(0) * tr
        c = lax.broadcasted_iota(jnp.int32, (tr, tc), 1) + pl.program_id(1) * tc
        w = jnp.where((r < rows) & (c < cols), w, 0.0)
    o_ref[...] = w.astype(o_ref.dtype)


def _cast_layer(w, layer, rows_pad, cols_pad):
    _, rows, cols = w.shape
    tc = next(c for c in (2048, 1024, 512) if cols_pad % c == 0)
    tr = CAST_TILE_ELEMS // tc
    assert rows_pad % tr == 0
    return pl.pallas_call(
        functools.partial(_cast_kernel, rows=rows, cols=cols),
        grid=(rows_pad // tr, cols_pad // tc),
        in_specs=[pl.BlockSpec((None, tr, tc), lambda i, j: (layer, i, j))],
        out_specs=pl.BlockSpec((tr, tc), lambda i, j: (i, j)),
        out_shape=jax.ShapeDtypeStruct((rows_pad, cols_pad), BF16),
        compiler_params=_params(("parallel", "parallel"), 2 * tr * tc * (4 + 2) + 4 * MIB),
        name="cast",
    )(w)


def _ffn_kernel(x_ref, g_ref, wg_ref, wu_ref, wd_ref, o_ref, xn_ref):
    f = pl.program_id(1)

    @pl.when(f == 0)
    def _():
        xn_ref[...] = _rms_norm_rows(x_ref[...], g_ref[...]).astype(BF16)
        o_ref[...] = jnp.zeros_like(o_ref)

    xn = xn_ref[...]
    gate = jnp.dot(xn, wg_ref[...], preferred_element_type=F32)
    up = jnp.dot(xn, wu_ref[...], preferred_element_type=F32)
    h = (jax.nn.silu(gate) * up).astype(BF16)
    o_ref[...] += jnp.dot(h, wd_ref[...], preferred_element_type=F32)

    @pl.when(f == pl.num_programs(1) - 1)
    def _():
        o_ref[...] = x_ref[...] + 0.5 * o_ref[...]


def _ffn(x, gain, wg, wu, wd):
    T, D = x.shape
    tm, tf = FFN_ROWS, FF_TILE
    vmem = 2 * (2 * tm * D * 4) + tm * D * 2 + 2 * 3 * D * tf * 2 + 3 * tm * tf * 4 + 2 * MIB
    return pl.pallas_call(
        _ffn_kernel,
        grid=(T // tm, D_FF_PAD // tf),
        in_specs=[
            pl.BlockSpec((tm, D), lambda i, f: (i, 0)),
            pl.BlockSpec((1, D), lambda i, f: (0, 0)),
            pl.BlockSpec((D, tf), lambda i, f: (0, f)),
            pl.BlockSpec((D, tf), lambda i, f: (0, f)),
            pl.BlockSpec((tf, D), lambda i, f: (f, 0)),
        ],
        out_specs=pl.BlockSpec((tm, D), lambda i, f: (i, 0)),
        out_shape=jax.ShapeDtypeStruct((T, D), F32),
        scratch_shapes=[pltpu.VMEM((tm, D), BF16)],
        compiler_params=_params(("parallel", "arbitrary"), vmem),
        name="ffn",
    )(x, gain, wg, wu, wd)


def _proj_kernel(x_ref, g_ref, w_ref, qg_ref, kg_ref, ca_ref, sa_ref, cr_ref, sr_ref, o_ref, xn_ref):
    n = pl.program_id(1)

    @pl.when(n == 0)
    def _():
        xn_ref[...] = _rms_norm_rows(x_ref[...], g_ref[...]).astype(BF16)

    def by_chunks(epilogue):
        xn = xn_ref[...]
        for j in range(SEG // PROJ_CHUNK):
            c0 = j * PROJ_CHUNK
            y = jnp.dot(xn, w_ref[:, c0:c0 + PROJ_CHUNK], preferred_element_type=F32)
            epilogue(y, c0)

    def attn_qk(gain_ref):
        def epilogue(y, c0):
            cos, sin = ca_ref[...], sa_ref[...]
            for h0 in range(0, PROJ_CHUNK, ATTN_HEAD_DIM):
                yn = _rms_norm_rows(y[:, h0:h0 + ATTN_HEAD_DIM], gain_ref[...])
                o_ref[:, c0 + h0:c0 + h0 + ATTN_HEAD_DIM] = yn * cos + pltpu.roll(yn, ATTN_HEAD_DIM // 2, 1) * sin
        by_chunks(epilogue)

    def ret_qk(scale):
        def epilogue(y, c0):
            cos, sin = cr_ref[...], sr_ref[...]
            half = RET_DIM // 2
            lo, hi = y[:, :half], y[:, half:]
            o_ref[:, c0:c0 + half] = (lo * cos[:, :half] + hi * sin[:, :half]) * scale
            o_ref[:, c0 + half:c0 + RET_DIM] = (hi * cos[:, half:] + lo * sin[:, half:]) * scale
        by_chunks(epilogue)

    def plain():
        def epilogue(y, c0):
            o_ref[:, c0:c0 + PROJ_CHUNK] = y
        by_chunks(epilogue)

    @pl.when(n == 0)
    def _():
        attn_qk(qg_ref)

    @pl.when(n == 1)
    def _():
        attn_qk(kg_ref)

    @pl.when(n == 3)
    def _():
        ret_qk(1.0)

    @pl.when(n == 4)
    def _():
        ret_qk(RET_DIM ** -0.5)

    @pl.when((n == 2) | (n >= 5))
    def _():
        plain()


def _rope_tables(seq, head_dim):
    half = head_dim // 2
    inv_freq = ROPE_THETA ** (-jnp.arange(half, dtype=F32) / half)
    ang = jnp.arange(seq, dtype=F32)[:, None] * inv_freq[None, :]
    cos, sin = jnp.cos(ang), jnp.sin(ang)
    return jnp.concatenate([cos, cos], axis=-1), jnp.concatenate([-sin, sin], axis=-1)


def _proj(x, gain, w_in, q_gain, k_gain, tables, seq):
    T, D = x.shape
    tm = PROJ_ROWS
    nseq = seq // tm
    ca, sa, cr, sr = tables
    pos = lambda i, n: (i % nseq, 0)
    const = lambda i, n: (0, 0)
    vmem = 2 * tm * D * 4 + tm * D * 2 + 2 * D * SEG * 2 + 2 * tm * SEG * 4 + 2 * 2 * tm * (128 + 256) * 4 \
        + 6 * tm * PROJ_CHUNK * 4 + 4 * MIB
    return pl.pallas_call(
        _proj_kernel,
        grid=(T // tm, IN_COLS // SEG),
        in_specs=[
            pl.BlockSpec((tm, D), lambda i, n: (i, 0)),
            pl.BlockSpec((1, D), const),
            pl.BlockSpec((D, SEG), lambda i, n: (0, n)),
            pl.BlockSpec((1, ATTN_HEAD_DIM), const),
            pl.BlockSpec((1, ATTN_HEAD_DIM), const),
            pl.BlockSpec((tm, ATTN_HEAD_DIM), pos),
            pl.BlockSpec((tm, ATTN_HEAD_DIM), pos),
            pl.BlockSpec((tm, RET_DIM), pos),
            pl.BlockSpec((tm, RET_DIM), pos),
        ],
        out_specs=pl.BlockSpec((tm, SEG), lambda i, n: (i, n)),
        out_shape=jax.ShapeDtypeStruct((T, IN_COLS), F32),
        scratch_shapes=[pltpu.VMEM((tm, D), BF16)],
        compiler_params=_params(("parallel", "arbitrary"), vmem),
        name="proj",
    )(x, gain, w_in, q_gain, k_gain, ca, sa, cr, sr)


def _attn_kernel(q_ref, k_ref, v_ref, o_ref, po_ref, pl_ref, bias_ref, *, seq):
    scale = ATTN_HEAD_DIM ** -0.5
    exp2_scale = scale * LOG2_E
    blocks_per_super = ATTN_SUPER // ATTN_QBLK

    rel = (lax.broadcasted_iota(jnp.int32, (ATTN_QBLK, ATTN_KWIN), 0)
           - lax.broadcasted_iota(jnp.int32, (ATTN_QBLK, ATTN_KWIN), 1))
    for case in range(3):
        bias_ref[case] = jnp.where(jnp.abs(rel + case * ATTN_HALF) <= ATTN_HALF, 0.0, NEG_INF)

    def super_block(sb, carry):
        base = pl.multiple_of(sb * ATTN_SUPER, ATTN_SUPER)
        for p, d in enumerate(DILATIONS):
            sub_len = seq // d
            per_res = blocks_per_super // d

            def block(u, c, p=p, d=d, sub_len=sub_len, per_res=per_res):
                r = u % d
                jl = u // d
                jb = sb * per_res + jl
                q0 = jb * ATTN_QBLK
                k0 = jnp.clip(q0 - ATTN_HALF, 0, sub_len - ATTN_KWIN)
                if d == 1:
                    qs, ks = pl.ds(q0, ATTN_QBLK), pl.ds(k0, ATTN_KWIN)
                    os = pl.ds(jl * ATTN_QBLK, ATTN_QBLK)
                else:
                    qs = pl.ds(r + d * q0, ATTN_QBLK, stride=d)
                    ks = pl.ds(r + d * k0, ATTN_KWIN, stride=d)
                    os = pl.ds(r + d * jl * ATTN_QBLK, ATTN_QBLK, stride=d)
                q = q_ref[qs, :].astype(BF16)
                k = k_ref[ks, :].astype(BF16)
                v = v_ref[ks, :].astype(BF16)
                s = lax.dot_general(q, k, (((1,), (1,)), ((), ())), preferred_element_type=F32)
                s = s + bias_ref[(q0 - k0) // ATTN_HALF]
                m = jnp.max(s, axis=-1, keepdims=True)
                e = jnp.exp2((s - m) * exp2_scale)
                den = jnp.sum(e, axis=-1, keepdims=True)
                out = jnp.dot(e.astype(BF16), v, preferred_element_type=F32) / den
                po_ref[p, os, :] = out
                pl_ref[p, os, :] = jnp.broadcast_to(m * scale + jnp.log(den), (ATTN_QBLK, ATTN_HEAD_DIM))
                return c

            lax.fori_loop(0, blocks_per_super, block, 0, unroll=ATTN_UNROLL)

        def merge(i, c):
            rows = pl.ds(pl.multiple_of(i * ATTN_QBLK, ATTN_QBLK), ATTN_QBLK)
            lse = [pl_ref[p, rows, :] for p in range(len(DILATIONS))]
            top = functools.reduce(jnp.maximum, lse)
            w = [jnp.exp(l - top) for l in lse]
            tot = functools.reduce(jnp.add, w)
            acc = functools.reduce(jnp.add, [(w[p] / tot) * po_ref[p, rows, :] for p in range(len(DILATIONS))])
            o_ref[pl.ds(base + i * ATTN_QBLK, ATTN_QBLK), :] = acc.astype(o_ref.dtype)
            return c

        lax.fori_loop(0, blocks_per_super, merge, 0)
        return carry

    lax.fori_loop(0, seq // ATTN_SUPER, super_block, 0)


def _attn(p3):
    B, S, _ = p3.shape
    assert S % ATTN_SUPER == 0 and S // DILATIONS[-1] >= ATTN_KWIN
    blk = (None, S, ATTN_HEAD_DIM)
    vmem = 2 * 3 * S * 128 * 4 + 2 * S * 128 * 2 + 2 * len(DILATIONS) * ATTN_SUPER * 128 * 4 + 8 * MIB
    return pl.pallas_call(
        functools.partial(_attn_kernel, seq=S),
        grid=(B, ATTN_HEADS),
        in_specs=[
            pl.BlockSpec(blk, lambda b, h: (b, 0, h)),
            pl.BlockSpec(blk, lambda b, h: (b, 0, ATTN_HEADS + h)),
            pl.BlockSpec(blk, lambda b, h: (b, 0, 2 * ATTN_HEADS + h)),
        ],
        out_specs=pl.BlockSpec(blk, lambda b, h: (b, 0, h)),
        out_shape=jax.ShapeDtypeStruct((B, S, ATTN_WIDTH), BF16),
        scratch_shapes=[pltpu.VMEM((len(DILATIONS), ATTN_SUPER, ATTN_HEAD_DIM), F32),
                        pltpu.VMEM((len(DILATIONS), ATTN_SUPER, ATTN_HEAD_DIM), F32),
                        pltpu.VMEM((3, ATTN_QBLK, ATTN_KWIN), F32)],
        compiler_params=_params(("parallel", "parallel"), vmem),
        name="attn",
    )(p3, p3, p3)


def _decay_tables(lg, backward):
    C = RET_CHUNK
    i = lax.broadcasted_iota(jnp.int32, (C, C), 0)
    j = lax.broadcasted_iota(jnp.int32, (C, C), 1)
    diff = (j - i) if backward else (i - j)
    mask = jnp.where(diff >= 0, jnp.exp(lg * jnp.maximum(diff, 0).astype(F32)), 0.0)
    idx = lax.broadcasted_iota(jnp.int32, (C, 1), 0).astype(F32)
    if backward:
        k_pow, q_pow = idx, C - idx
    else:
        k_pow, q_pow = C - 1 - idx, idx + 1
    chunk = jnp.exp(jnp.full((1, RET_DIM), C, F32) * lg)
    return mask, jnp.exp(lg * k_pow), jnp.exp(lg * q_pow), chunk


def _ret_cross(q, k, vb, state, k_scale, q_scale, chunk_decay):
    cross = jnp.dot((q * q_scale).astype(BF16), state.astype(BF16), preferred_element_type=F32)
    kv = lax.dot_general((k * k_scale).astype(BF16), vb, (((0,), (0,)), ((), ())), preferred_element_type=F32)
    return cross, chunk_decay * state + kv


def _ret_bwd_kernel(ld_ref, q_ref, k_ref, v_ref, o_ref, state_ref):
    h, t = pl.program_id(1), pl.program_id(2)

    @pl.when(t == 0)
    def _():
        state_ref[...] = jnp.zeros_like(state_ref)

    _, k_scale, q_scale, chunk_decay = _decay_tables(ld_ref[h], backward=True)
    state = state_ref[...]
    for c in reversed(range(RET_ROWS // RET_CHUNK)):
        rows = slice(c * RET_CHUNK, (c + 1) * RET_CHUNK)
        cross, state = _ret_cross(q_ref[rows, :], k_ref[rows, :], v_ref[rows, :].astype(BF16), state,
                                  k_scale, q_scale, chunk_decay)
        o_ref[rows, :] = cross
    state_ref[...] = state


def _ret_fwd_kernel(ldf_ref, ldb_ref, q_ref, k_ref, v_ref, gate_ref, bwd_ref, gain_ref, o_ref, state_ref):
    h, t = pl.program_id(1), pl.program_id(2)

    @pl.when(t == 0)
    def _():
        state_ref[...] = jnp.zeros_like(state_ref)

    mask_f, k_scale, q_scale, chunk_decay = _decay_tables(ldf_ref[h], backward=False)
    mask = mask_f + _decay_tables(ldb_ref[h], backward=True)[0]
    state = state_ref[...]
    for c in range(RET_ROWS // RET_CHUNK):
        rows = slice(c * RET_CHUNK, (c + 1) * RET_CHUNK)
        q, k, vb = q_ref[rows, :], k_ref[rows, :], v_ref[rows, :].astype(BF16)
        s = lax.dot_general(q.astype(BF16), k.astype(BF16), (((1,), (1,)), ((), ())), preferred_element_type=F32)
        inner = jnp.dot((s * mask).astype(BF16), vb, preferred_element_type=F32)
        cross, state = _ret_cross(q, k, vb, state, k_scale, q_scale, chunk_decay)
        both = _rms_norm_rows(inner + cross + bwd_ref[rows, :], gain_ref[...])
        o_ref[rows, :] = (jax.nn.silu(gate_ref[rows, :]) * both).astype(o_ref.dtype)
    state_ref[...] = state


def _ret_specs(n_t, reverse):
    tmap = (lambda t: n_t - 1 - t) if reverse else (lambda t: t)
    blk = (None, RET_ROWS, RET_DIM)
    seg0 = 3 * ATTN_WIDTH // RET_DIM

    def col(seg):
        return pl.BlockSpec(blk, lambda b, h, t: (b, tmap(t), seg0 + seg * RET_HEADS + h))

    return blk, tmap, col


def _ret_bwd(p3, log_decay):
    B, S, _ = p3.shape
    n_t = S // RET_ROWS
    blk, tmap, col = _ret_specs(n_t, reverse=True)
    vmem = 2 * 4 * RET_ROWS * RET_DIM * 4 + RET_DIM * RET_DIM * 4 + 8 * MIB
    return pl.pallas_call(
        _ret_bwd_kernel,
        grid=(B, RET_HEADS, n_t),
        in_specs=[pl.BlockSpec(memory_space=pltpu.SMEM), col(0), col(1), col(2)],
        out_specs=pl.BlockSpec(blk, lambda b, h, t: (b, tmap(t), h)),
        out_shape=jax.ShapeDtypeStruct((B, S, RET_WIDTH), F32),
        scratch_shapes=[pltpu.VMEM((RET_DIM, RET_DIM), F32)],
        compiler_params=_params(("parallel", "parallel", "arbitrary"), vmem),
        name="ret_bwd",
    )(log_decay, p3, p3, p3)


def _ret_fwd(p3, bwd, log_decay_fwd, log_decay_bwd, out_gain):
    B, S, _ = p3.shape
    n_t = S // RET_ROWS
    blk, tmap, col = _ret_specs(n_t, reverse=False)
    vmem = 2 * 6 * RET_ROWS * RET_DIM * 4 + RET_DIM * RET_DIM * 4 + 8 * MIB
    smem = pl.BlockSpec(memory_space=pltpu.SMEM)
    return pl.pallas_call(
        _ret_fwd_kernel,
        grid=(B, RET_HEADS, n_t),
        in_specs=[smem, smem, col(0), col(1), col(2), col(3),
                  pl.BlockSpec(blk, lambda b, h, t: (b, t, h)),
                  pl.BlockSpec((None, 1, RET_DIM), lambda b, h, t: (h, 0, 0))],
        out_specs=pl.BlockSpec(blk, lambda b, h, t: (b, t, h)),
        out_shape=jax.ShapeDtypeStruct((B, S, RET_WIDTH), BF16),
        scratch_shapes=[pltpu.VMEM((RET_DIM, RET_DIM), F32)],
        compiler_params=_params(("parallel", "parallel", "arbitrary"), vmem),
        name="ret_fwd",
    )(log_decay_fwd, log_decay_bwd, p3, p3, p3, p3, bwd, out_gain)


def _mix_out_kernel(x_ref, a_ref, r_ref, w_ref, o_ref):
    acc = jnp.dot(a_ref[...], w_ref[:ATTN_WIDTH, :], preferred_element_type=F32)
    acc += jnp.dot(r_ref[...], w_ref[ATTN_WIDTH:, :], preferred_element_type=F32)
    o_ref[...] = x_ref[...] + acc


def _mix_out(x, a, r, w_out):
    T, D = x.shape
    tm = ROW_TILE
    vmem = 2 * (2 * tm * D * 4 + 2 * tm * SEG * 2 + D * D * 2) + tm * D * 4 + 4 * MIB
    return pl.pallas_call(
        _mix_out_kernel,
        grid=(T // tm,),
        in_specs=[
            pl.BlockSpec((tm, D), lambda i: (i, 0)),
            pl.BlockSpec((tm, ATTN_WIDTH), lambda i: (i, 0)),
            pl.BlockSpec((tm, RET_WIDTH), lambda i: (i, 0)),
            pl.BlockSpec((D, D), lambda i: (0, 0)),
        ],
        out_specs=pl.BlockSpec((tm, D), lambda i: (i, 0)),
        out_shape=jax.ShapeDtypeStruct((T, D), F32),
        compiler_params=_params(("parallel",), vmem),
        name="mix_out",
    )(x, a, r, w_out)


def _trunk(x3, layers):
    B, S, D = x3.shape
    T = B * S
    x = x3.reshape(T, D)
    tables = _rope_tables(S, ATTN_HEAD_DIM) + _rope_tables(S, RET_DIM)
    for lw in layers:
        x = _ffn(x, lw["ffn1_norm"], lw["ffn1_wg"], lw["ffn1_wu"], lw["ffn1_wd"])
        proj = _proj(x, lw["mix_norm"], lw["w_in"], lw["q_gain"], lw["k_gain"], tables, S)
        p3 = proj.reshape(B, S, IN_COLS)
        a = _attn(p3)
        bwd = _ret_bwd(p3, lw["ld_bwd"])
        r = _ret_fwd(p3, bwd, lw["ld_fwd"], lw["ld_bwd"], lw["ret_gain"])
        x = _mix_out(x, a.reshape(T, ATTN_WIDTH), r.reshape(T, RET_WIDTH), lw["w_out"])
        x = _ffn(x, lw["ffn2_norm"], lw["ffn2_wg"], lw["ffn2_wu"], lw["ffn2_wd"])
    return x.reshape(B, S, D)


def kernel(x_prompt, x_sample, ffn1_norm, ffn1_w_gate, ffn1_w_up, ffn1_w_down, mix_norm, w_in, attn_q_norm, attn_k_norm, ret_log_decay_fwd, ret_log_decay_bwd, ret_out_norm, w_out, ffn2_norm, ffn2_w_gate, ffn2_w_up, ffn2_w_down):
    depth = w_in.shape[0]
    layers = []
    for l in range(depth):
        layers.append({
            "ffn1_norm": ffn1_norm[l][None, :].astype(F32),
            "ffn1_wg": _cast_layer(ffn1_w_gate, l, D_MODEL, D_FF_PAD),
            "ffn1_wu": _cast_layer(ffn1_w_up, l, D_MODEL, D_FF_PAD),
            "ffn1_wd": _cast_layer(ffn1_w_down, l, D_FF_PAD, D_MODEL),
            "mix_norm": mix_norm[l][None, :].astype(F32),
            "w_in": _cast_layer(w_in, l, D_MODEL, IN_COLS),
            "q_gain": attn_q_norm[l][None, :].astype(F32),
            "k_gain": attn_k_norm[l][None, :].astype(F32),
            "ld_fwd": ret_log_decay_fwd[l].astype(F32),
            "ld_bwd": ret_log_decay_bwd[l].astype(F32),
            "ret_gain": ret_out_norm[l][:, None, :].astype(F32),
            "w_out": _cast_layer(w_out, l, D_MODEL, D_MODEL),
            "ffn2_norm": ffn2_norm[l][None, :].astype(F32),
            "ffn2_wg": _cast_layer(ffn2_w_gate, l, D_MODEL, D_FF_PAD),
            "ffn2_wu": _cast_layer(ffn2_w_up, l, D_MODEL, D_FF_PAD),
            "ffn2_wd": _cast_layer(ffn2_w_down, l, D_FF_PAD, D_MODEL),
        })
    return (_trunk(x_prompt, layers), _trunk(x_sample, layers))
```

```python
import functools

import jax
import jax.numpy as jnp
from jax import lax
from jax.experimental import pallas as pl
from jax.experimental.pallas import tpu as pltpu

F32 = jnp.float32
BF16 = jnp.bfloat16

D_MODEL = 2048
ATTN_HEAD_DIM = 128
ATTN_HEADS = 8
ATTN_WIDTH = ATTN_HEADS * ATTN_HEAD_DIM
DILATIONS = (1, 4, 16)
ATTN_HALF = 64
RET_HEADS = 4
RET_DIM = 256
RET_WIDTH = RET_HEADS * RET_DIM
RET_CHUNK = 128
D_FF = 5504
IN_COLS = 3 * ATTN_WIDTH + 4 * RET_WIDTH
ROPE_THETA = 10000.0
NORM_EPS = 1e-6
NEG_INF = -1e30
LOG2_E = 1.4426950408889634

V7X_LANES = 128
FF_TILE = 512
D_FF_PAD = -(-D_FF // FF_TILE) * FF_TILE
ROW_TILE = 512
FFN_ROWS = 1024
PROJ_ROWS = 1024
CAST_TILE_ELEMS = 1024 * 1024
SEG = 1024
PROJ_CHUNK = 256
ATTN_QBLK = 128
ATTN_KWIN = 256
ATTN_SUPER = ATTN_QBLK * DILATIONS[-1]
ATTN_UNROLL = 16
RET_ROWS = 1024
MIB = 1024 * 1024


def _params(semantics, vmem_bytes):
    return pltpu.CompilerParams(dimension_semantics=semantics, vmem_limit_bytes=int(vmem_bytes))


def _rms_norm_rows(x, gain):
    return x * lax.rsqrt(jnp.mean(x * x, axis=-1, keepdims=True) + NORM_EPS) * gain


def _cast_kernel(w_ref, o_ref, *, rows, cols):
    tr, tc = o_ref.shape
    w = w_ref[...]
    if rows % tr or cols % tc:
        r = lax.broadcasted_iota(jnp.int32, (tr, tc), 0) + pl.program_id(0) * tr
        c = lax.broadcasted_iota(jnp.int32, (tr, tc), 1) + pl.program_id(1) * tc
        w = jnp.where((r < rows) & (c < cols), w, 0.0)
    o_ref[...] = w.astype(o_ref.dtype)


def _cast_layer(w, layer, rows_pad, cols_pad):
    _, rows, cols = w.shape
    tc = next(c for c in (2048, 1024, 512) if cols_pad % c == 0)
    tr = CAST_TILE_ELEMS // tc
    assert rows_pad % tr == 0
    return pl.pallas_call(
        functools.partial(_cast_kernel, rows=rows, cols=cols),
        grid=(rows_pad // tr, cols_pad // tc),
        in_specs=[pl.BlockSpec((None, tr, tc), lambda i, j: (layer, i, j))],
        out_specs=pl.BlockSpec((tr, tc), lambda i, j: (i, j)),
        out_shape=jax.ShapeDtypeStruct((rows_pad, cols_pad), BF16),
        compiler_params=_params(("parallel", "parallel"), 2 * tr * tc * (4 + 2) + 4 * MIB),
        name="cast",
    )(w)


def _ffn_kernel(x_ref, g_ref, wg_ref, wu_ref, wd_ref, o_ref, xn_ref):
    f = pl.program_id(1)

    @pl.when(f == 0)
    def _():
        xn_ref[...] = _rms_norm_rows(x_ref[...], g_ref[...]).astype(BF16)
        o_ref[...] = jnp.zeros_like(o_ref)

    xn = xn_ref[...]
    gate = jnp.dot(xn, wg_ref[...], preferred_element_type=F32)
    up = jnp.dot(xn, wu_ref[...], preferred_element_type=F32)
    h = (jax.nn.silu(gate) * up).astype(BF16)
    o_ref[...] += jnp.dot(h, wd_ref[...], preferred_element_type=F32)

    @pl.when(f == pl.num_programs(1) - 1)
    def _():
        o_ref[...] = x_ref[...] + 0.5 * o_ref[...]


def _ffn(x, gain, wg, wu, wd):
    T, D = x.shape
    tm, tf = FFN_ROWS, FF_TILE
    vmem = 2 * (2 * tm * D * 4) + tm * D * 2 + 2 * 3 * D * tf * 2 + 3 * tm * tf * 4 + 2 * MIB
    return pl.pallas_call(
        _ffn_kernel,
        grid=(T // tm, D_FF_PAD // tf),
        in_specs=[
            pl.BlockSpec((tm, D), lambda i, f: (i, 0)),
            pl.BlockSpec((1, D), lambda i, f: (0, 0)),
            pl.BlockSpec((D, tf), lambda i, f: (0, f)),
            pl.BlockSpec((D, tf), lambda i, f: (0, f)),
            pl.BlockSpec((tf, D), lambda i, f: (f, 0)),
        ],
        out_specs=pl.BlockSpec((tm, D), lambda i, f: (i, 0)),
        out_shape=jax.ShapeDtypeStruct((T, D), F32),
        scratch_shapes=[pltpu.VMEM((tm, D), BF16)],
        compiler_params=_params(("parallel", "arbitrary"), vmem),
        name="ffn",
    )(x, gain, wg, wu, wd)


def _proj_kernel(x_ref, g_ref, w_ref, cr_ref, sr_ref, o_ref, xn_ref):
    n = pl.program_id(1)

    @pl.when(n == 0)
    def _():
        xn_ref[...] = _rms_norm_rows(x_ref[...], g_ref[...]).astype(BF16)

    def by_chunks(epilogue):
        xn = xn_ref[...]
        for j in range(SEG // PROJ_CHUNK):
            c0 = j * PROJ_CHUNK
            y = jnp.dot(xn, w_ref[:, c0:c0 + PROJ_CHUNK], preferred_element_type=F32)
            epilogue(y, c0)

    def ret_qk(scale):
        def epilogue(y, c0):
            cos, sin = cr_ref[...], sr_ref[...]
            half = RET_DIM // 2
            lo, hi = y[:, :half], y[:, half:]
            o_ref[:, c0:c0 + half] = (lo * cos[:, :half] + hi * sin[:, :half]) * scale
            o_ref[:, c0 + half:c0 + RET_DIM] = (hi * cos[:, half:] + lo * sin[:, half:]) * scale
        by_chunks(epilogue)

    def plain():
        def epilogue(y, c0):
            o_ref[:, c0:c0 + PROJ_CHUNK] = y
        by_chunks(epilogue)

    @pl.when(n == 3)
    def _():
        ret_qk(1.0)

    @pl.when(n == 4)
    def _():
        ret_qk(RET_DIM ** -0.5)

    @pl.when((n < 3) | (n >= 5))
    def _():
        plain()


def _rope_tables(seq, head_dim):
    half = head_dim // 2
    inv_freq = ROPE_THETA ** (-jnp.arange(half, dtype=F32) / half)
    ang = jnp.arange(seq, dtype=F32)[:, None] * inv_freq[None, :]
    cos, sin = jnp.cos(ang), jnp.sin(ang)
    return jnp.concatenate([cos, cos], axis=-1), jnp.concatenate([-sin, sin], axis=-1)


def _proj(x, gain, w_in, ret_tables, seq):
    T, D = x.shape
    tm = PROJ_ROWS
    nseq = seq // tm
    cr, sr = ret_tables
    pos = lambda i, n: (i % nseq, 0)
    const = lambda i, n: (0, 0)
    vmem = 2 * tm * D * 4 + tm * D * 2 + 2 * D * SEG * 2 + 2 * tm * SEG * 4 + 2 * 2 * tm * RET_DIM * 4 \
        + 6 * tm * PROJ_CHUNK * 4 + 4 * MIB
    return pl.pallas_call(
        _proj_kernel,
        grid=(T // tm, IN_COLS // SEG),
        in_specs=[
            pl.BlockSpec((tm, D), lambda i, n: (i, 0)),
            pl.BlockSpec((1, D), const),
            pl.BlockSpec((D, SEG), lambda i, n: (0, n)),
            pl.BlockSpec((tm, RET_DIM), pos),
            pl.BlockSpec((tm, RET_DIM), pos),
        ],
        out_specs=pl.BlockSpec((tm, SEG), lambda i, n: (i, n)),
        out_shape=jax.ShapeDtypeStruct((T, IN_COLS), F32),
        scratch_shapes=[pltpu.VMEM((tm, D), BF16)],
        compiler_params=_params(("parallel", "arbitrary"), vmem),
        name="proj",
    )(x, gain, w_in, cr, sr)


def _attn_kernel(qraw_ref, kraw_ref, v_ref, qg_ref, kg_ref, cos_ref, sin_ref, o_ref,
                 q_ref, k_ref, po_ref, pl_ref, bias_ref, *, seq):
    scale = ATTN_HEAD_DIM ** -0.5
    exp2_scale = scale * LOG2_E
    blocks_per_super = ATTN_SUPER // ATTN_QBLK

    rel = (lax.broadcasted_iota(jnp.int32, (ATTN_QBLK, ATTN_KWIN), 0)
           - lax.broadcasted_iota(jnp.int32, (ATTN_QBLK, ATTN_KWIN), 1))
    for case in range(3):
        bias_ref[case] = jnp.where(jnp.abs(rel + case * ATTN_HALF) <= ATTN_HALF, 0.0, NEG_INF)

    def qk_norm_rope(i, c):
        rows = pl.ds(pl.multiple_of(i * ATTN_KWIN, ATTN_KWIN), ATTN_KWIN)
        cos, sin = cos_ref[rows, :], sin_ref[rows, :]
        for raw_ref, gain_ref, dst_ref in ((qraw_ref, qg_ref, q_ref), (kraw_ref, kg_ref, k_ref)):
            yn = _rms_norm_rows(raw_ref[rows, :], gain_ref[...])
            dst_ref[rows, :] = yn * cos + pltpu.roll(yn, ATTN_HEAD_DIM // 2, 1) * sin
        return c

    lax.fori_loop(0, seq // ATTN_KWIN, qk_norm_rope, 0, unroll=2)

    def super_block(sb, carry):
        base = pl.multiple_of(sb * ATTN_SUPER, ATTN_SUPER)
        for p, d in enumerate(DILATIONS):
            sub_len = seq // d
            per_res = blocks_per_super // d

            def block(u, c, p=p, d=d, sub_len=sub_len, per_res=per_res):
                r = u % d
                jl = u // d
                jb = sb * per_res + jl
                q0 = jb * ATTN_QBLK
                k0 = jnp.clip(q0 - ATTN_HALF, 0, sub_len - ATTN_KWIN)
                if d == 1:
                    qs, ks = pl.ds(q0, ATTN_QBLK), pl.ds(k0, ATTN_KWIN)
                    os = pl.ds(jl * ATTN_QBLK, ATTN_QBLK)
                else:
                    qs = pl.ds(r + d * q0, ATTN_QBLK, stride=d)
                    ks = pl.ds(r + d * k0, ATTN_KWIN, stride=d)
                    os = pl.ds(r + d * jl * ATTN_QBLK, ATTN_QBLK, stride=d)
                q = q_ref[qs, :].astype(BF16)
                k = k_ref[ks, :].astype(BF16)
                v = v_ref[ks, :].astype(BF16)
                s = lax.dot_general(q, k, (((1,), (1,)), ((), ())), preferred_element_type=F32)
                s = s + bias_ref[(q0 - k0) // ATTN_HALF]
                m = jnp.max(s, axis=-1, keepdims=True)
                e = jnp.exp2((s - m) * exp2_scale)
                den = jnp.sum(e, axis=-1, keepdims=True)
                out = jnp.dot(e.astype(BF16), v, preferred_element_type=F32) / den
                po_ref[p, os, :] = out
                pl_ref[p, os, :] = jnp.broadcast_to(m * scale + jnp.log(den), (ATTN_QBLK, ATTN_HEAD_DIM))
                return c

            lax.fori_loop(0, blocks_per_super, block, 0, unroll=ATTN_UNROLL)

        def merge(i, c):
            rows = pl.ds(pl.multiple_of(i * ATTN_QBLK, ATTN_QBLK), ATTN_QBLK)
            lse = [pl_ref[p, rows, :] for p in range(len(DILATIONS))]
            top = functools.reduce(jnp.maximum, lse)
            w = [jnp.exp(l - top) for l in lse]
            tot = functools.reduce(jnp.add, w)
            acc = functools.reduce(jnp.add, [(w[p] / tot) * po_ref[p, rows, :] for p in range(len(DILATIONS))])
            o_ref[pl.ds(base + i * ATTN_QBLK, ATTN_QBLK), :] = acc.astype(o_ref.dtype)
            return c

        lax.fori_loop(0, blocks_per_super, merge, 0)
        return carry

    lax.fori_loop(0, seq // ATTN_SUPER, super_block, 0)


def _attn(p3, q_gain, k_gain, tables):
    B, S, _ = p3.shape
    assert S % ATTN_SUPER == 0 and S // DILATIONS[-1] >= ATTN_KWIN
    blk = (None, S, ATTN_HEAD_DIM)
    const = lambda b, h: (0, 0)
    once = dict(pipeline_mode=pl.Buffered(1))
    vmem = 2 * 3 * S * 128 * 4 + 2 * S * 128 * 2 + 4 * S * 128 * 4 + 2 * len(DILATIONS) * ATTN_SUPER * 128 * 4 \
        + 8 * MIB
    return pl.pallas_call(
        functools.partial(_attn_kernel, seq=S),
        grid=(B, ATTN_HEADS),
        in_specs=[
            pl.BlockSpec(blk, lambda b, h: (b, 0, h)),
            pl.BlockSpec(blk, lambda b, h: (b, 0, ATTN_HEADS + h)),
            pl.BlockSpec(blk, lambda b, h: (b, 0, 2 * ATTN_HEADS + h)),
            pl.BlockSpec((1, ATTN_HEAD_DIM), const),
            pl.BlockSpec((1, ATTN_HEAD_DIM), const),
            pl.BlockSpec((S, ATTN_HEAD_DIM), const, **once),
            pl.BlockSpec((S, ATTN_HEAD_DIM), const, **once),
        ],
        out_specs=pl.BlockSpec(blk, lambda b, h: (b, 0, h)),
        out_shape=jax.ShapeDtypeStruct((B, S, ATTN_WIDTH), BF16),
        scratch_shapes=[pltpu.VMEM((S, ATTN_HEAD_DIM), F32),
                        pltpu.VMEM((S, ATTN_HEAD_DIM), F32),
                        pltpu.VMEM((len(DILATIONS), ATTN_SUPER, ATTN_HEAD_DIM), F32),
                        pltpu.VMEM((len(DILATIONS), ATTN_SUPER, ATTN_HEAD_DIM), F32),
                        pltpu.VMEM((3, ATTN_QBLK, ATTN_KWIN), F32)],
        compiler_params=_params(("parallel", "parallel"), vmem),
        name="attn",
    )(p3, p3, p3, q_gain, k_gain, *tables)


def _decay_tables(lg, backward):
    C = RET_CHUNK
    i = lax.broadcasted_iota(jnp.int32, (C, C), 0)
    j = lax.broadcasted_iota(jnp.int32, (C, C), 1)
    diff = (j - i) if backward else (i - j)
    mask = jnp.where(diff >= 0, jnp.exp(lg * jnp.maximum(diff, 0).astype(F32)), 0.0)
    idx = lax.broadcasted_iota(jnp.int32, (C, 1), 0).astype(F32)
    if backward:
        k_pow, q_pow = idx, C - idx
    else:
        k_pow, q_pow = C - 1 - idx, idx + 1
    chunk = jnp.exp(jnp.full((1, RET_DIM), C, F32) * lg)
    return mask, jnp.exp(lg * k_pow), jnp.exp(lg * q_pow), chunk


def _ret_cross(q, k, vb, state, k_scale, q_scale, chunk_decay):
    cross = jnp.dot((q * q_scale).astype(BF16), state.astype(BF16), preferred_element_type=F32)
    kv = lax.dot_general((k * k_scale).astype(BF16), vb, (((0,), (0,)), ((), ())), preferred_element_type=F32)
    return cross, chunk_decay * state + kv


def _ret_bwd_kernel(ld_ref, q_ref, k_ref, v_ref, o_ref, state_ref):
    h, t = pl.program_id(1), pl.program_id(2)

    @pl.when(t == 0)
    def _():
        state_ref[...] = jnp.zeros_like(state_ref)

    _, k_scale, q_scale, chunk_decay = _decay_tables(ld_ref[h], backward=True)
    state = state_ref[...]
    for c in reversed(range(RET_ROWS // RET_CHUNK)):
        rows = slice(c * RET_CHUNK, (c + 1) * RET_CHUNK)
        cross, state = _ret_cross(q_ref[rows, :], k_ref[rows, :], v_ref[rows, :].astype(BF16), state,
                                  k_scale, q_scale, chunk_decay)
        o_ref[rows, :] = cross
    state_ref[...] = state


def _ret_fwd_kernel(ldf_ref, ldb_ref, q_ref, k_ref, v_ref, gate_ref, bwd_ref, gain_ref, o_ref, state_ref):
    h, t = pl.program_id(1), pl.program_id(2)

    @pl.when(t == 0)
    def _():
        state_ref[...] = jnp.zeros_like(state_ref)

    mask_f, k_scale, q_scale, chunk_decay = _decay_tables(ldf_ref[h], backward=False)
    mask = mask_f + _decay_tables(ldb_ref[h], backward=True)[0]
    state = state_ref[...]
    for c in range(RET_ROWS // RET_CHUNK):
        rows = slice(c * RET_CHUNK, (c + 1) * RET_CHUNK)
        q, k, vb = q_ref[rows, :], k_ref[rows, :], v_ref[rows, :].astype(BF16)
        s = lax.dot_general(q.astype(BF16), k.astype(BF16), (((1,), (1,)), ((), ())), preferred_element_type=F32)
        inner = jnp.dot((s * mask).astype(BF16), vb, preferred_element_type=F32)
        cross, state = _ret_cross(q, k, vb, state, k_scale, q_scale, chunk_decay)
        both = _rms_norm_rows(inner + cross + bwd_ref[rows, :], gain_ref[...])
        o_ref[rows, :] = (jax.nn.silu(gate_ref[rows, :]) * both).astype(o_ref.dtype)
    state_ref[...] = state


def _ret_specs(n_t, reverse):
    tmap = (lambda t: n_t - 1 - t) if reverse else (lambda t: t)
    blk = (None, RET_ROWS, RET_DIM)
    seg0 = 3 * ATTN_WIDTH // RET_DIM

    def col(seg):
        return pl.BlockSpec(blk, lambda b, h, t: (b, tmap(t), seg0 + seg * RET_HEADS + h))

    return blk, tmap, col


def _ret_bwd(p3, log_decay):
    B, S, _ = p3.shape
    n_t = S // RET_ROWS
    blk, tmap, col = _ret_specs(n_t, reverse=True)
    vmem = 2 * 4 * RET_ROWS * RET_DIM * 4 + RET_DIM * RET_DIM * 4 + 8 * MIB
    return pl.pallas_call(
        _ret_bwd_kernel,
        grid=(B, RET_HEADS, n_t),
        in_specs=[pl.BlockSpec(memory_space=pltpu.SMEM), col(0), col(1), col(2)],
        out_specs=pl.BlockSpec(blk, lambda b, h, t: (b, tmap(t), h)),
        out_shape=jax.ShapeDtypeStruct((B, S, RET_WIDTH), F32),
        scratch_shapes=[pltpu.VMEM((RET_DIM, RET_DIM), F32)],
        compiler_params=_params(("parallel", "parallel", "arbitrary"), vmem),
        name="ret_bwd",
    )(log_decay, p3, p3, p3)


def _ret_fwd(p3, bwd, log_decay_fwd, log_decay_bwd, out_gain):
    B, S, _ = p3.shape
    n_t = S // RET_ROWS
    blk, tmap, col = _ret_specs(n_t, reverse=False)
    vmem = 2 * 6 * RET_ROWS * RET_DIM * 4 + RET_DIM * RET_DIM * 4 + 8 * MIB
    smem = pl.BlockSpec(memory_space=pltpu.SMEM)
    return pl.pallas_call(
        _ret_fwd_kernel,
        grid=(B, RET_HEADS, n_t),
        in_specs=[smem, smem, col(0), col(1), col(2), col(3),
                  pl.BlockSpec(blk, lambda b, h, t: (b, t, h)),
                  pl.BlockSpec((None, 1, RET_DIM), lambda b, h, t: (h, 0, 0))],
        out_specs=pl.BlockSpec(blk, lambda b, h, t: (b, t, h)),
        out_shape=jax.ShapeDtypeStruct((B, S, RET_WIDTH), BF16),
        scratch_shapes=[pltpu.VMEM((RET_DIM, RET_DIM), F32)],
        compiler_params=_params(("parallel", "parallel", "arbitrary"), vmem),
        name="ret_fwd",
    )(log_decay_fwd, log_decay_bwd, p3, p3, p3, p3, bwd, out_gain)


def _mix_out_kernel(x_ref, a_ref, r_ref, w_ref, o_ref):
    acc = jnp.dot(a_ref[...], w_ref[:ATTN_WIDTH, :], preferred_element_type=F32)
    acc += jnp.dot(r_ref[...], w_ref[ATTN_WIDTH:, :], preferred_element_type=F32)
    o_ref[...] = x_ref[...] + acc


def _mix_out(x, a, r, w_out):
    T, D = x.shape
    tm = ROW_TILE
    vmem = 2 * (2 * tm * D * 4 + 2 * tm * SEG * 2 + D * D * 2) + tm * D * 4 + 4 * MIB
    return pl.pallas_call(
        _mix_out_kernel,
        grid=(T // tm,),
        in_specs=[
            pl.BlockSpec((tm, D), lambda i: (i, 0)),
            pl.BlockSpec((tm, ATTN_WIDTH), lambda i: (i, 0)),
            pl.BlockSpec((tm, RET_WIDTH), lambda i: (i, 0)),
            pl.BlockSpec((D, D), lambda i: (0, 0)),
        ],
        out_specs=pl.BlockSpec((tm, D), lambda i: (i, 0)),
        out_shape=jax.ShapeDtypeStruct((T, D), F32),
        compiler_params=_params(("parallel",), vmem),
        name="mix_out",
    )(x, a, r, w_out)


def _trunk(x3, layers):
    B, S, D = x3.shape
    T = B * S
    x = x3.reshape(T, D)
    attn_tables, ret_tables = _rope_tables(S, ATTN_HEAD_DIM), _rope_tables(S, RET_DIM)
    for lw in layers:
        x = _ffn(x, lw["ffn1_norm"], lw["ffn1_wg"], lw["ffn1_wu"], lw["ffn1_wd"])
        proj = _proj(x, lw["mix_norm"], lw["w_in"], ret_tables, S)
        p3 = proj.reshape(B, S, IN_COLS)
        a = _attn(p3, lw["q_gain"], lw["k_gain"], attn_tables)
        bwd = _ret_bwd(p3, lw["ld_bwd"])
        r = _ret_fwd(p3, bwd, lw["ld_fwd"], lw["ld_bwd"], lw["ret_gain"])
        x = _mix_out(x, a.reshape(T, ATTN_WIDTH), r.reshape(T, RET_WIDTH), lw["w_out"])
        x = _ffn(x, lw["ffn2_norm"], lw["ffn2_wg"], lw["ffn2_wu"], lw["ffn2_wd"])
    return x.reshape(B, S, D)


def kernel(x_prompt, x_sample, ffn1_norm, ffn1_w_gate, ffn1_w_up, ffn1_w_down, mix_norm, w_in, attn_q_norm, attn_k_norm, ret_log_decay_fwd, ret_log_decay_bwd, ret_out_norm, w_out, ffn2_norm, ffn2_w_gate, ffn2_w_up, ffn2_w_down):
    depth = w_in.shape[0]
    layers = []
    for l in range(depth):
        layers.append({
            "ffn1_norm": ffn1_norm[l][None, :].astype(F32),
            "ffn1_wg": _cast_layer(ffn1_w_gate, l, D_MODEL, D_FF_PAD),
            "ffn1_wu": _cast_layer(ffn1_w_up, l, D_MODEL, D_FF_PAD),
            "ffn1_wd": _cast_layer(ffn1_w_down, l, D_FF_PAD, D_MODEL),
            "mix_norm": mix_norm[l][None, :].astype(F32),
            "w_in": _cast_layer(w_in, l, D_MODEL, IN_COLS),
            "q_gain": attn_q_norm[l][None, :].astype(F32),
            "k_gain": attn_k_norm[l][None, :].astype(F32),
            "ld_fwd": ret_log_decay_fwd[l].astype(F32),
            "ld_bwd": ret_log_decay_bwd[l].astype(F32),
            "ret_gain": ret_out_norm[l][:, None, :].astype(F32),
            "w_out": _cast_layer(w_out, l, D_MODEL, D_MODEL),
            "ffn2_norm": ffn2_norm[l][None, :].astype(F32),
            "ffn2_wg": _cast_layer(ffn2_w_gate, l, D_MODEL, D_FF_PAD),
            "ffn2_wu": _cast_layer(ffn2_w_up, l, D_MODEL, D_FF_PAD),
            "ffn2_wd": _cast_layer(ffn2_w_down, l, D_FF_PAD, D_MODEL),
        })
    return (_trunk(x_prompt, layers), _trunk(x_sample, layers))
```

```python
import functools

import jax
import jax.numpy as jnp
from jax import lax
from jax.experimental import pallas as pl
from jax.experimental.pallas import tpu as pltpu

F32 = jnp.float32
BF16 = jnp.bfloat16

D_MODEL = 2048
ATTN_HEAD_DIM = 128
ATTN_HEADS = 8
ATTN_WIDTH = ATTN_HEADS * ATTN_HEAD_DIM
DILATIONS = (1, 4, 16)
ATTN_HALF = 64
RET_HEADS = 4
RET_DIM = 256
RET_WIDTH = RET_HEADS * RET_DIM
RET_CHUNK = 128
D_FF = 5504
IN_COLS = 3 * ATTN_WIDTH + 4 * RET_WIDTH
ROPE_THETA = 10000.0
NORM_EPS = 1e-6
NEG_INF = -1e30
LOG2_E = 1.4426950408889634

V7X_LANES = 128
V7X_SCOPED_VMEM_BYTES = 56 * 1024 * 1024
FF_TILE = 512
D_FF_PAD = -(-D_FF // FF_TILE) * FF_TILE
ROW_TILE = 512
FFN_ROWS = 1024
PROJ_ROWS = 1024
CAST_TILE_ELEMS = 1024 * 1024
SEG = 1024
PROJ_CHUNK = 256
ATTN_QBLK = 128
ATTN_KWIN = 256
ATTN_SUPER = ATTN_QBLK * DILATIONS[-1]
ATTN_UNROLL = 16
RET_ROWS = 1024
MIB = 1024 * 1024


def _params(semantics, vmem_bytes, claim_all=False):
    if claim_all:
        assert vmem_bytes <= V7X_SCOPED_VMEM_BYTES
        vmem_bytes = V7X_SCOPED_VMEM_BYTES
    return pltpu.CompilerParams(dimension_semantics=semantics, vmem_limit_bytes=int(vmem_bytes))


def _rms_norm_rows(x, gain):
    return x * lax.rsqrt(jnp.mean(x * x, axis=-1, keepdims=True) + NORM_EPS) * gain


def _cast_kernel(w_ref, o_ref, *, rows, cols):
    tr, tc = o_ref.shape
    w = w_ref[...]
    if rows % tr or cols % tc:
        r = lax.broadcasted_iota(jnp.int32, (tr, tc), 0) + pl.program_id(0) * tr
        c = lax.broadcasted_iota(jnp.int32, (tr, tc), 1) + pl.program_id(1) * tc
        w = jnp.where((r < rows) & (c < cols), w, 0.0)
    o_ref[...] = w.astype(o_ref.dtype)


def _cast_layer(w, layer, rows_pad, cols_pad):
    _, rows, cols = w.shape
    tc = next(c for c in (2048, 1024, 512) if cols_pad % c == 0)
    tr = CAST_TILE_ELEMS // tc
    assert rows_pad % tr == 0
    return pl.pallas_call(
        functools.partial(_cast_kernel, rows=rows, cols=cols),
        grid=(rows_pad // tr, cols_pad // tc),
        in_specs=[pl.BlockSpec((None, tr, tc), lambda i, j: (layer, i, j))],
        out_specs=pl.BlockSpec((tr, tc), lambda i, j: (i, j)),
        out_shape=jax.ShapeDtypeStruct((rows_pad, cols_pad), BF16),
        compiler_params=_params(("parallel", "parallel"), 2 * tr * tc * (4 + 2) + 4 * MIB, claim_all=True),
        name="cast",
    )(w)


def _ffn_kernel(x_ref, g_ref, wg_ref, wu_ref, wd_ref, o_ref, xn_ref):
    f = pl.program_id(1)

    @pl.when(f == 0)
    def _():
        xn_ref[...] = _rms_norm_rows(x_ref[...], g_ref[...]).astype(BF16)
        o_ref[...] = jnp.zeros_like(o_ref)

    xn = xn_ref[...]
    gate = jnp.dot(xn, wg_ref[...], preferred_element_type=F32)
    up = jnp.dot(xn, wu_ref[...], preferred_element_type=F32)
    h = (jax.nn.silu(gate) * up).astype(BF16)
    o_ref[...] += jnp.dot(h, wd_ref[...], preferred_element_type=F32)

    @pl.when(f == pl.num_programs(1) - 1)
    def _():
        o_ref[...] = x_ref[...] + 0.5 * o_ref[...]


def _ffn(x, gain, wg, wu, wd):
    T, D = x.shape
    tm, tf = FFN_ROWS, FF_TILE
    vmem = 2 * (2 * tm * D * 4) + tm * D * 2 + 2 * 3 * D * tf * 2 + 3 * tm * tf * 4 + 2 * MIB
    return pl.pallas_call(
        _ffn_kernel,
        grid=(T // tm, D_FF_PAD // tf),
        in_specs=[
            pl.BlockSpec((tm, D), lambda i, f: (i, 0)),
            pl.BlockSpec((1, D), lambda i, f: (0, 0)),
            pl.BlockSpec((D, tf), lambda i, f: (0, f)),
            pl.BlockSpec((D, tf), lambda i, f: (0, f)),
            pl.BlockSpec((tf, D), lambda i, f: (f, 0)),
        ],
        out_specs=pl.BlockSpec((tm, D), lambda i, f: (i, 0)),
        out_shape=jax.ShapeDtypeStruct((T, D), F32),
        scratch_shapes=[pltpu.VMEM((tm, D), BF16)],
        compiler_params=_params(("parallel", "arbitrary"), vmem),
        name="ffn",
    )(x, gain, wg, wu, wd)


def _proj_kernel(x_ref, g_ref, w_ref, cr_ref, sr_ref, o_ref, xn_ref):
    n = pl.program_id(1)

    @pl.when(n == 0)
    def _():
        xn_ref[...] = _rms_norm_rows(x_ref[...], g_ref[...]).astype(BF16)

    def by_chunks(epilogue):
        xn = xn_ref[...]
        for j in range(SEG // PROJ_CHUNK):
            c0 = j * PROJ_CHUNK
            y = jnp.dot(xn, w_ref[:, c0:c0 + PROJ_CHUNK], preferred_element_type=F32)
            epilogue(y, c0)

    def ret_qk(scale):
        def epilogue(y, c0):
            cos, sin = cr_ref[...], sr_ref[...]
            half = RET_DIM // 2
            lo, hi = y[:, :half], y[:, half:]
            o_ref[:, c0:c0 + half] = (lo * cos[:, :half] + hi * sin[:, :half]) * scale
            o_ref[:, c0 + half:c0 + RET_DIM] = (hi * cos[:, half:] + lo * sin[:, half:]) * scale
        by_chunks(epilogue)

    def plain():
        def epilogue(y, c0):
            o_ref[:, c0:c0 + PROJ_CHUNK] = y
        by_chunks(epilogue)

    @pl.when(n == 3)
    def _():
        ret_qk(1.0)

    @pl.when(n == 4)
    def _():
        ret_qk(RET_DIM ** -0.5)

    @pl.when((n < 3) | (n >= 5))
    def _():
        plain()


def _rope_tables(seq, head_dim):
    half = head_dim // 2
    inv_freq = ROPE_THETA ** (-jnp.arange(half, dtype=F32) / half)
    ang = jnp.arange(seq, dtype=F32)[:, None] * inv_freq[None, :]
    cos, sin = jnp.cos(ang), jnp.sin(ang)
    return jnp.concatenate([cos, cos], axis=-1), jnp.concatenate([-sin, sin], axis=-1)


def _proj(x, gain, w_in, ret_tables, seq):
    T, D = x.shape
    tm = PROJ_ROWS
    nseq = seq // tm
    cr, sr = ret_tables
    pos = lambda i, n: (i % nseq, 0)
    const = lambda i, n: (0, 0)
    vmem = 2 * tm * D * 4 + tm * D * 2 + 2 * D * SEG * 2 + 2 * tm * SEG * 4 + 2 * 2 * tm * RET_DIM * 4 \
        + 6 * tm * PROJ_CHUNK * 4 + 4 * MIB
    return pl.pallas_call(
        _proj_kernel,
        grid=(T // tm, IN_COLS // SEG),
        in_specs=[
            pl.BlockSpec((tm, D), lambda i, n: (i, 0)),
            pl.BlockSpec((1, D), const),
            pl.BlockSpec((D, SEG), lambda i, n: (0, n)),
            pl.BlockSpec((tm, RET_DIM), pos),
            pl.BlockSpec((tm, RET_DIM), pos),
        ],
        out_specs=pl.BlockSpec((tm, SEG), lambda i, n: (i, n)),
        out_shape=jax.ShapeDtypeStruct((T, IN_COLS), F32),
        scratch_shapes=[pltpu.VMEM((tm, D), BF16)],
        compiler_params=_params(("parallel", "arbitrary"), vmem),
        name="proj",
    )(x, gain, w_in, cr, sr)


def _attn_kernel(qraw_ref, kraw_ref, v_ref, qg_ref, kg_ref, cos_ref, sin_ref, o_ref,
                 q_ref, k_ref, po_ref, pl_ref, bias_ref, *, seq):
    scale = ATTN_HEAD_DIM ** -0.5
    exp2_scale = scale * LOG2_E
    blocks_per_super = ATTN_SUPER // ATTN_QBLK

    rel = (lax.broadcasted_iota(jnp.int32, (ATTN_QBLK, ATTN_KWIN), 0)
           - lax.broadcasted_iota(jnp.int32, (ATTN_QBLK, ATTN_KWIN), 1))
    for case in range(3):
        bias_ref[case] = jnp.where(jnp.abs(rel + case * ATTN_HALF) <= ATTN_HALF, 0.0, NEG_INF)

    ones = jnp.ones((ATTN_HEAD_DIM, ATTN_HEAD_DIM), BF16)

    def row_sums(t):
        hi = t.astype(BF16)
        lo = (t - hi.astype(F32)).astype(BF16)
        return jnp.dot(hi, ones, preferred_element_type=F32) + jnp.dot(lo, ones, preferred_element_type=F32)

    def qk_norm_rope(i, c):
        rows = pl.ds(pl.multiple_of(i * ATTN_KWIN, ATTN_KWIN), ATTN_KWIN)
        cos, sin = cos_ref[rows, :], sin_ref[rows, :]
        for raw_ref, gain_ref, dst_ref in ((qraw_ref, qg_ref, q_ref), (kraw_ref, kg_ref, k_ref)):
            y = raw_ref[rows, :]
            mean_sq = row_sums(y * y) * (1.0 / ATTN_HEAD_DIM)
            yn = y * lax.rsqrt(mean_sq + NORM_EPS) * gain_ref[...]
            dst_ref[rows, :] = yn * cos + pltpu.roll(yn, ATTN_HEAD_DIM // 2, 1) * sin
        return c

    lax.fori_loop(0, seq // ATTN_KWIN, qk_norm_rope, 0, unroll=2)

    def super_block(sb, carry):
        base = pl.multiple_of(sb * ATTN_SUPER, ATTN_SUPER)
        for p, d in enumerate(DILATIONS):
            sub_len = seq // d
            per_res = blocks_per_super // d

            def block(u, c, p=p, d=d, sub_len=sub_len, per_res=per_res):
                r = u % d
                jl = u // d
                jb = sb * per_res + jl
                q0 = jb * ATTN_QBLK
                k0 = jnp.clip(q0 - ATTN_HALF, 0, sub_len - ATTN_KWIN)
                if d == 1:
                    qs, ks = pl.ds(q0, ATTN_QBLK), pl.ds(k0, ATTN_KWIN)
                    os = pl.ds(jl * ATTN_QBLK, ATTN_QBLK)
                else:
                    qs = pl.ds(r + d * q0, ATTN_QBLK, stride=d)
                    ks = pl.ds(r + d * k0, ATTN_KWIN, stride=d)
                    os = pl.ds(r + d * jl * ATTN_QBLK, ATTN_QBLK, stride=d)
                q = q_ref[qs, :].astype(BF16)
                k = k_ref[ks, :].astype(BF16)
                v = v_ref[ks, :].astype(BF16)
                s = lax.dot_general(q, k, (((1,), (1,)), ((), ())), preferred_element_type=F32)
                s = s + bias_ref[(q0 - k0) // ATTN_HALF]
                m = jnp.max(s, axis=-1, keepdims=True)
                e = jnp.exp2((s - m) * exp2_scale)
                den = jnp.sum(e, axis=-1, keepdims=True)
                out = jnp.dot(e.astype(BF16), v, preferred_element_type=F32) / den
                po_ref[p, os, :] = out
                pl_ref[p, os, :] = jnp.broadcast_to(m * scale + jnp.log(den), (ATTN_QBLK, ATTN_HEAD_DIM))
                return c

            lax.fori_loop(0, blocks_per_super, block, 0, unroll=ATTN_UNROLL)

        def merge(i, c):
            rows = pl.ds(pl.multiple_of(i * ATTN_QBLK, ATTN_QBLK), ATTN_QBLK)
            lse = [pl_ref[p, rows, :] for p in range(len(DILATIONS))]
            top = functools.reduce(jnp.maximum, lse)
            w = [jnp.exp(l - top) for l in lse]
            tot = functools.reduce(jnp.add, w)
            acc = functools.reduce(jnp.add, [(w[p] / tot) * po_ref[p, rows, :] for p in range(len(DILATIONS))])
            o_ref[pl.ds(base + i * ATTN_QBLK, ATTN_QBLK), :] = acc.astype(o_ref.dtype)
            return c

        lax.fori_loop(0, blocks_per_super, merge, 0)
        return carry

    lax.fori_loop(0, seq // ATTN_SUPER, super_block, 0)


def _attn(p3, q_gain, k_gain, tables):
    B, S, _ = p3.shape
    assert S % ATTN_SUPER == 0 and S // DILATIONS[-1] >= ATTN_KWIN
    blk = (None, S, ATTN_HEAD_DIM)
    const = lambda b, h: (0, 0)
    once = dict(pipeline_mode=pl.Buffered(1))
    vmem = 2 * 3 * S * 128 * 4 + 2 * S * 128 * 2 + 4 * S * 128 * 4 + 2 * len(DILATIONS) * ATTN_SUPER * 128 * 4 \
        + 4 * MIB
    return pl.pallas_call(
        functools.partial(_attn_kernel, seq=S),
        grid=(B, ATTN_HEADS),
        in_specs=[
            pl.BlockSpec(blk, lambda b, h: (b, 0, h)),
            pl.BlockSpec(blk, lambda b, h: (b, 0, ATTN_HEADS + h)),
            pl.BlockSpec(blk, lambda b, h: (b, 0, 2 * ATTN_HEADS + h)),
            pl.BlockSpec((1, ATTN_HEAD_DIM), const),
            pl.BlockSpec((1, ATTN_HEAD_DIM), const),
            pl.BlockSpec((S, ATTN_HEAD_DIM), const, **once),
            pl.BlockSpec((S, ATTN_HEAD_DIM), const, **once),
        ],
        out_specs=pl.BlockSpec(blk, lambda b, h: (b, 0, h)),
        out_shape=jax.ShapeDtypeStruct((B, S, ATTN_WIDTH), BF16),
        scratch_shapes=[pltpu.VMEM((S, ATTN_HEAD_DIM), F32),
                        pltpu.VMEM((S, ATTN_HEAD_DIM), F32),
                        pltpu.VMEM((len(DILATIONS), ATTN_SUPER, ATTN_HEAD_DIM), F32),
                        pltpu.VMEM((len(DILATIONS), ATTN_SUPER, ATTN_HEAD_DIM), F32),
                        pltpu.VMEM((3, ATTN_QBLK, ATTN_KWIN), F32)],
        compiler_params=_params(("parallel", "parallel"), vmem, claim_all=True),
        name="attn",
    )(p3, p3, p3, q_gain, k_gain, *tables)


def _decay_tables(lg, backward):
    C = RET_CHUNK
    i = lax.broadcasted_iota(jnp.int32, (C, C), 0)
    j = lax.broadcasted_iota(jnp.int32, (C, C), 1)
    diff = (j - i) if backward else (i - j)
    mask = jnp.where(diff >= 0, jnp.exp(lg * jnp.maximum(diff, 0).astype(F32)), 0.0)
    idx = lax.broadcasted_iota(jnp.int32, (C, 1), 0).astype(F32)
    if backward:
        k_pow, q_pow = idx, C - idx
    else:
        k_pow, q_pow = C - 1 - idx, idx + 1
    chunk = jnp.exp(jnp.full((1, RET_DIM), C, F32) * lg)
    return mask, jnp.exp(lg * k_pow), jnp.exp(lg * q_pow), chunk


def _ret_cross(q, k, vb, state, k_scale, q_scale, chunk_decay):
    cross = jnp.dot((q * q_scale).astype(BF16), state.astype(BF16), preferred_element_type=F32)
    kv = lax.dot_general((k * k_scale).astype(BF16), vb, (((0,), (0,)), ((), ())), preferred_element_type=F32)
    return cross, chunk_decay * state + kv


def _ret_bwd_kernel(ld_ref, q_ref, k_ref, v_ref, o_ref, state_ref):
    h, t = pl.program_id(1), pl.program_id(2)

    @pl.when(t == 0)
    def _():
        state_ref[...] = jnp.zeros_like(state_ref)

    _, k_scale, q_scale, chunk_decay = _decay_tables(ld_ref[h], backward=True)
    state = state_ref[...]
    for c in reversed(range(RET_ROWS // RET_CHUNK)):
        rows = slice(c * RET_CHUNK, (c + 1) * RET_CHUNK)
        cross, state = _ret_cross(q_ref[rows, :], k_ref[rows, :], v_ref[rows, :].astype(BF16), state,
                                  k_scale, q_scale, chunk_decay)
        o_ref[rows, :] = cross
    state_ref[...] = state


def _ret_fwd_kernel(ldf_ref, ldb_ref, q_ref, k_ref, v_ref, gate_ref, bwd_ref, gain_ref, o_ref, state_ref):
    h, t = pl.program_id(1), pl.program_id(2)

    @pl.when(t == 0)
    def _():
        state_ref[...] = jnp.zeros_like(state_ref)

    mask_f, k_scale, q_scale, chunk_decay = _decay_tables(ldf_ref[h], backward=False)
    mask = mask_f + _decay_tables(ldb_ref[h], backward=True)[0]
    state = state_ref[...]
    for c in range(RET_ROWS // RET_CHUNK):
        rows = slice(c * RET_CHUNK, (c + 1) * RET_CHUNK)
        q, k, vb = q_ref[rows, :], k_ref[rows, :], v_ref[rows, :].astype(BF16)
        s = lax.dot_general(q.astype(BF16), k.astype(BF16), (((1,), (1,)), ((), ())), preferred_element_type=F32)
        inner = jnp.dot((s * mask).astype(BF16), vb, preferred_element_type=F32)
        cross, state = _ret_cross(q, k, vb, state, k_scale, q_scale, chunk_decay)
        both = _rms_norm_rows(inner + cross + bwd_ref[rows, :], gain_ref[...])
        o_ref[rows, :] = (jax.nn.silu(gate_ref[rows, :]) * both).astype(o_ref.dtype)
    state_ref[...] = state


def _ret_specs(n_t, reverse):
    tmap = (lambda t: n_t - 1 - t) if reverse else (lambda t: t)
    blk = (None, RET_ROWS, RET_DIM)
    seg0 = 3 * ATTN_WIDTH // RET_DIM

    def col(seg):
        return pl.BlockSpec(blk, lambda b, h, t: (b, tmap(t), seg0 + seg * RET_HEADS + h))

    return blk, tmap, col


def _ret_bwd(p3, log_decay):
    B, S, _ = p3.shape
    n_t = S // RET_ROWS
    blk, tmap, col = _ret_specs(n_t, reverse=True)
    vmem = 2 * 4 * RET_ROWS * RET_DIM * 4 + RET_DIM * RET_DIM * 4 + 8 * MIB
    return pl.pallas_call(
        _ret_bwd_kernel,
        grid=(B, RET_HEADS, n_t),
        in_specs=[pl.BlockSpec(memory_space=pltpu.SMEM), col(0), col(1), col(2)],
        out_specs=pl.BlockSpec(blk, lambda b, h, t: (b, tmap(t), h)),
        out_shape=jax.ShapeDtypeStruct((B, S, RET_WIDTH), F32),
        scratch_shapes=[pltpu.VMEM((RET_DIM, RET_DIM), F32)],
        compiler_params=_params(("parallel", "parallel", "arbitrary"), vmem, claim_all=True),
        name="ret_bwd",
    )(log_decay, p3, p3, p3)


def _ret_fwd(p3, bwd, log_decay_fwd, log_decay_bwd, out_gain):
    B, S, _ = p3.shape
    n_t = S // RET_ROWS
    blk, tmap, col = _ret_specs(n_t, reverse=False)
    vmem = 2 * 6 * RET_ROWS * RET_DIM * 4 + RET_DIM * RET_DIM * 4 + 8 * MIB
    smem = pl.BlockSpec(memory_space=pltpu.SMEM)
    return pl.pallas_call(
        _ret_fwd_kernel,
        grid=(B, RET_HEADS, n_t),
        in_specs=[smem, smem, col(0), col(1), col(2), col(3),
                  pl.BlockSpec(blk, lambda b, h, t: (b, t, h)),
                  pl.BlockSpec((None, 1, RET_DIM), lambda b, h, t: (h, 0, 0))],
        out_specs=pl.BlockSpec(blk, lambda b, h, t: (b, t, h)),
        out_shape=jax.ShapeDtypeStruct((B, S, RET_WIDTH), BF16),
        scratch_shapes=[pltpu.VMEM((RET_DIM, RET_DIM), F32)],
        compiler_params=_params(("parallel", "parallel", "arbitrary"), vmem, claim_all=True),
        name="ret_fwd",
    )(log_decay_fwd, log_decay_bwd, p3, p3, p3, p3, bwd, out_gain)


def _mix_out_kernel(x_ref, a_ref, r_ref, w_ref, o_ref):
    acc = jnp.dot(a_ref[...], w_ref[:ATTN_WIDTH, :], preferred_element_type=F32)
    acc += jnp.dot(r_ref[...], w_ref[ATTN_WIDTH:, :], preferred_element_type=F32)
    o_ref[...] = x_ref[...] + acc


def _mix_out(x, a, r, w_out):
    T, D = x.shape
    tm = ROW_TILE
    vmem = 2 * (2 * tm * D * 4 + 2 * tm * SEG * 2 + D * D * 2) + tm * D * 4 + 4 * MIB
    return pl.pallas_call(
        _mix_out_kernel,
        grid=(T // tm,),
        in_specs=[
            pl.BlockSpec((tm, D), lambda i: (i, 0)),
            pl.BlockSpec((tm, ATTN_WIDTH), lambda i: (i, 0)),
            pl.BlockSpec((tm, RET_WIDTH), lambda i: (i, 0)),
            pl.BlockSpec((D, D), lambda i: (0, 0)),
        ],
        out_specs=pl.BlockSpec((tm, D), lambda i: (i, 0)),
        out_shape=jax.ShapeDtypeStruct((T, D), F32),
        compiler_params=_params(("parallel",), vmem, claim_all=True),
        name="mix_out",
    )(x, a, r, w_out)


def _trunk(x3, layers):
    B, S, D = x3.shape
    T = B * S
    x = x3.reshape(T, D)
    attn_tables, ret_tables = _rope_tables(S, ATTN_HEAD_DIM), _rope_tables(S, RET_DIM)
    for lw in layers:
        x = _ffn(x, lw["ffn1_norm"], lw["ffn1_wg"], lw["ffn1_wu"], lw["ffn1_wd"])
        proj = _proj(x, lw["mix_norm"], lw["w_in"], ret_tables, S)
        p3 = proj.reshape(B, S, IN_COLS)
        a = _attn(p3, lw["q_gain"], lw["k_gain"], attn_tables)
        bwd = _ret_bwd(p3, lw["ld_bwd"])
        r = _ret_fwd(p3, bwd, lw["ld_fwd"], lw["ld_bwd"], lw["ret_gain"])
        x = _mix_out(x, a.reshape(T, ATTN_WIDTH), r.reshape(T, RET_WIDTH), lw["w_out"])
        x = _ffn(x, lw["ffn2_norm"], lw["ffn2_wg"], lw["ffn2_wu"], lw["ffn2_wd"])
    return x.reshape(B, S, D)


def kernel(x_prompt, x_sample, ffn1_norm, ffn1_w_gate, ffn1_w_up, ffn1_w_down, mix_norm, w_in, attn_q_norm, attn_k_norm, ret_log_decay_fwd, ret_log_decay_bwd, ret_out_norm, w_out, ffn2_norm, ffn2_w_gate, ffn2_w_up, ffn2_w_down):
    depth = w_in.shape[0]
    layers = []
    for l in range(depth):
        layers.append({
            "ffn1_norm": ffn1_norm[l][None, :].astype(F32),
            "ffn1_wg": _cast_layer(ffn1_w_gate, l, D_MODEL, D_FF_PAD),
            "ffn1_wu": _cast_layer(ffn1_w_up, l, D_MODEL, D_FF_PAD),
            "ffn1_wd": _cast_layer(ffn1_w_down, l, D_FF_PAD, D_MODEL),
            "mix_norm": mix_norm[l][None, :].astype(F32),
            "w_in": _cast_layer(w_in, l, D_MODEL, IN_COLS),
            "q_gain": attn_q_norm[l][None, :].astype(F32),
            "k_gain": attn_k_norm[l][None, :].astype(F32),
            "ld_fwd": ret_log_decay_fwd[l].astype(F32),
            "ld_bwd": ret_log_decay_bwd[l].astype(F32),
            "ret_gain": ret_out_norm[l][:, None, :].astype(F32),
            "w_out": _cast_layer(w_out, l, D_MODEL, D_MODEL),
            "ffn2_norm": ffn2_norm[l][None, :].astype(F32),
            "ffn2_wg": _cast_layer(ffn2_w_gate, l, D_MODEL, D_FF_PAD),
            "ffn2_wu": _cast_layer(ffn2_w_up, l, D_MODEL, D_FF_PAD),
            "ffn2_wd": _cast_layer(ffn2_w_down, l, D_FF_PAD, D_MODEL),
        })
    return (_trunk(x_prompt, layers), _trunk(x_sample, layers))
```

```python
import functools

import jax
import jax.numpy as jnp
from jax import lax
from jax.experimental import pallas as pl
from jax.experimental.pallas import tpu as pltpu

F32 = jnp.float32
BF16 = jnp.bfloat16

D_MODEL = 2048
ATTN_HEAD_DIM = 128
ATTN_HEADS = 8
ATTN_WIDTH = ATTN_HEADS * ATTN_HEAD_DIM
DILATIONS = (1, 4, 16)
ATTN_HALF = 64
RET_HEADS = 4
RET_DIM = 256
RET_WIDTH = RET_HEADS * RET_DIM
RET_CHUNK = 128
D_FF = 5504
IN_COLS = 3 * ATTN_WIDTH + 4 * RET_WIDTH
ROPE_THETA = 10000.0
NORM_EPS = 1e-6
NEG_INF = -1e30
LOG2_E = 1.4426950408889634

V7X_LANES = 128
V7X_SCOPED_VMEM_BYTES = 56 * 1024 * 1024
FF_TILE = 512
D_FF_PAD = -(-D_FF // FF_TILE) * FF_TILE
ROW_TILE = 512
FFN_ROWS = 1024
PROJ_ROWS = 1024
CAST_TILE_ELEMS = 1024 * 1024
SEG = 1024
PROJ_CHUNK = 256
ATTN_QBLK = 128
ATTN_KWIN = 256
ATTN_SUPER = ATTN_QBLK * DILATIONS[-1]
ATTN_UNROLL = 16
RET_ROWS = 1024
MIB = 1024 * 1024


def _params(semantics, vmem_bytes, claim_all=False):
    if claim_all:
        assert vmem_bytes <= V7X_SCOPED_VMEM_BYTES
        vmem_bytes = V7X_SCOPED_VMEM_BYTES
    return pltpu.CompilerParams(dimension_semantics=semantics, vmem_limit_bytes=int(vmem_bytes))


def _rms_norm_rows(x, gain):
    return x * lax.rsqrt(jnp.mean(x * x, axis=-1, keepdims=True) + NORM_EPS) * gain


def _cast_kernel(w_ref, o_ref, *, rows, cols):
    tr, tc = o_ref.shape
    w = w_ref[...]
    if rows % tr or cols % tc:
        r = lax.broadcasted_iota(jnp.int32, (tr, tc), 0) + pl.program_id(0) * tr
        c = lax.broadcasted_iota(jnp.int32, (tr, tc), 1) + pl.program_id(1) * tc
        w = jnp.where((r < rows) & (c < cols), w, 0.0)
    o_ref[...] = w.astype(o_ref.dtype)


def _cast_layer(w, layer, rows_pad, cols_pad):
    _, rows, cols = w.shape
    tc = next(c for c in (2048, 1024, 512) if cols_pad % c == 0)
    tr = CAST_TILE_ELEMS // tc
    assert rows_pad % tr == 0
    return pl.pallas_call(
        functools.partial(_cast_kernel, rows=rows, cols=cols),
        grid=(rows_pad // tr, cols_pad // tc),
        in_specs=[pl.BlockSpec((None, tr, tc), lambda i, j: (layer, i, j))],
        out_specs=pl.BlockSpec((tr, tc), lambda i, j: (i, j)),
        out_shape=jax.ShapeDtypeStruct((rows_pad, cols_pad), BF16),
        compiler_params=_params(("parallel", "parallel"), 2 * tr * tc * (4 + 2) + 4 * MIB, claim_all=True),
        name="cast",
    )(w)


def _ffn_kernel(x_ref, g_ref, wg_ref, wu_ref, wd_ref, o_ref, xn_ref):
    f = pl.program_id(1)
    last = pl.num_programs(1) - 1
    tm = x_ref.shape[0]

    def step(rows, first, final):
        if first:
            xn_ref[rows, :] = _rms_norm_rows(x_ref[rows, :], g_ref[...]).astype(BF16)
        xn = xn_ref[rows, :]
        gate = jnp.dot(xn, wg_ref[...], preferred_element_type=F32)
        up = jnp.dot(xn, wu_ref[...], preferred_element_type=F32)
        h = (jax.nn.silu(gate) * up).astype(BF16)
        acc = jnp.dot(h, wd_ref[...], preferred_element_type=F32)
        if not first:
            acc = o_ref[rows, :] + acc
        o_ref[rows, :] = x_ref[rows, :] + 0.5 * acc if final else acc

    halves = (slice(0, tm // 2), slice(tm // 2, tm))

    @pl.when(f == 0)
    def _():
        for rows in halves:
            step(rows, True, False)

    @pl.when((f > 0) & (f < last))
    def _():
        step(slice(0, tm), False, False)

    @pl.when(f == last)
    def _():
        for rows in halves:
            step(rows, False, True)


def _ffn(x, gain, wg, wu, wd):
    T, D = x.shape
    tm, tf = FFN_ROWS, FF_TILE
    vmem = 2 * (2 * tm * D * 4) + tm * D * 2 + 2 * 3 * D * tf * 2 + 3 * tm * tf * 4 + 2 * MIB
    return pl.pallas_call(
        _ffn_kernel,
        grid=(T // tm, D_FF_PAD // tf),
        in_specs=[
            pl.BlockSpec((tm, D), lambda i, f: (i, 0)),
            pl.BlockSpec((1, D), lambda i, f: (0, 0)),
            pl.BlockSpec((D, tf), lambda i, f: (0, f)),
            pl.BlockSpec((D, tf), lambda i, f: (0, f)),
            pl.BlockSpec((tf, D), lambda i, f: (f, 0)),
        ],
        out_specs=pl.BlockSpec((tm, D), lambda i, f: (i, 0)),
        out_shape=jax.ShapeDtypeStruct((T, D), F32),
        scratch_shapes=[pltpu.VMEM((tm, D), BF16)],
        compiler_params=_params(("parallel", "arbitrary"), vmem),
        name="ffn",
    )(x, gain, wg, wu, wd)


def _proj_kernel(x_ref, g_ref, w_ref, cr_ref, sr_ref, o_ref, xn_ref):
    n = pl.program_id(1)

    tm = x_ref.shape[0]

    def by_chunks(epilogue, rows=slice(None)):
        xn = xn_ref[rows, :]
        for j in range(SEG // PROJ_CHUNK):
            c0 = j * PROJ_CHUNK
            y = jnp.dot(xn, w_ref[:, c0:c0 + PROJ_CHUNK], preferred_element_type=F32)
            epilogue(y, c0, rows)

    def ret_qk(scale):
        def epilogue(y, c0, rows):
            cos, sin = cr_ref[rows, :], sr_ref[rows, :]
            half = RET_DIM // 2
            lo, hi = y[:, :half], y[:, half:]
            o_ref[rows, c0:c0 + half] = (lo * cos[:, :half] + hi * sin[:, :half]) * scale
            o_ref[rows, c0 + half:c0 + RET_DIM] = (hi * cos[:, half:] + lo * sin[:, half:]) * scale
        by_chunks(epilogue)

    def plain(rows=slice(None)):
        def epilogue(y, c0, rows):
            o_ref[rows, c0:c0 + PROJ_CHUNK] = y
        by_chunks(epilogue, rows)

    @pl.when(n == 0)
    def _():
        for rows in (slice(0, tm // 2), slice(tm // 2, tm)):
            xn_ref[rows, :] = _rms_norm_rows(x_ref[rows, :], g_ref[...]).astype(BF16)
            plain(rows)

    @pl.when(n == 3)
    def _():
        ret_qk(1.0)

    @pl.when(n == 4)
    def _():
        ret_qk(RET_DIM ** -0.5)

    @pl.when(((n > 0) & (n < 3)) | (n >= 5))
    def _():
        plain()


def _rope_tables(seq, head_dim):
    half = head_dim // 2
    inv_freq = ROPE_THETA ** (-jnp.arange(half, dtype=F32) / half)
    ang = jnp.arange(seq, dtype=F32)[:, None] * inv_freq[None, :]
    cos, sin = jnp.cos(ang), jnp.sin(ang)
    return jnp.concatenate([cos, cos], axis=-1), jnp.concatenate([-sin, sin], axis=-1)


def _proj(x, gain, w_in, ret_tables, seq):
    T, D = x.shape
    tm = PROJ_ROWS
    nseq = seq // tm
    cr, sr = ret_tables
    pos = lambda i, n: (i % nseq, 0)
    const = lambda i, n: (0, 0)
    vmem = 2 * tm * D * 4 + tm * D * 2 + 2 * D * SEG * 2 + 2 * tm * SEG * 4 + 2 * 2 * tm * RET_DIM * 4 \
        + 6 * tm * PROJ_CHUNK * 4 + 4 * MIB
    return pl.pallas_call(
        _proj_kernel,
        grid=(T // tm, IN_COLS // SEG),
        in_specs=[
            pl.BlockSpec((tm, D), lambda i, n: (i, 0)),
            pl.BlockSpec((1, D), const),
            pl.BlockSpec((D, SEG), lambda i, n: (0, n)),
            pl.BlockSpec((tm, RET_DIM), pos),
            pl.BlockSpec((tm, RET_DIM), pos),
        ],
        out_specs=pl.BlockSpec((tm, SEG), lambda i, n: (i, n)),
        out_shape=jax.ShapeDtypeStruct((T, IN_COLS), F32),
        scratch_shapes=[pltpu.VMEM((tm, D), BF16)],
        compiler_params=_params(("parallel", "arbitrary"), vmem),
        name="proj",
    )(x, gain, w_in, cr, sr)


def _attn_kernel(qraw_ref, kraw_ref, v_ref, qg_ref, kg_ref, cos_ref, sin_ref, o_ref,
                 q_ref, k_ref, po_ref, pl_ref, bias_ref, *, seq):
    scale = ATTN_HEAD_DIM ** -0.5
    exp2_scale = scale * LOG2_E
    blocks_per_super = ATTN_SUPER // ATTN_QBLK

    rel = (lax.broadcasted_iota(jnp.int32, (ATTN_QBLK, ATTN_KWIN), 0)
           - lax.broadcasted_iota(jnp.int32, (ATTN_QBLK, ATTN_KWIN), 1))
    for case in range(3):
        bias_ref[case] = jnp.where(jnp.abs(rel + case * ATTN_HALF) <= ATTN_HALF, 0.0, NEG_INF)

    ones = jnp.ones((ATTN_HEAD_DIM, ATTN_HEAD_DIM), BF16)

    def row_sums(t):
        hi = t.astype(BF16)
        lo = (t - hi.astype(F32)).astype(BF16)
        return jnp.dot(hi, ones, preferred_element_type=F32) + jnp.dot(lo, ones, preferred_element_type=F32)

    def qk_norm_rope(i, c):
        rows = pl.ds(pl.multiple_of(i * ATTN_KWIN, ATTN_KWIN), ATTN_KWIN)
        cos, sin = cos_ref[rows, :], sin_ref[rows, :]
        for raw_ref, gain_ref, dst_ref in ((qraw_ref, qg_ref, q_ref), (kraw_ref, kg_ref, k_ref)):
            y = raw_ref[rows, :]
            mean_sq = row_sums(y * y) * (1.0 / ATTN_HEAD_DIM)
            yn = y * lax.rsqrt(mean_sq + NORM_EPS) * gain_ref[...]
            dst_ref[rows, :] = yn * cos + pltpu.roll(yn, ATTN_HEAD_DIM // 2, 1) * sin
        return c

    lax.fori_loop(0, seq // ATTN_KWIN, qk_norm_rope, 0, unroll=4)

    def super_block(sb, carry):
        base = pl.multiple_of(sb * ATTN_SUPER, ATTN_SUPER)
        for p, d in enumerate(DILATIONS):
            sub_len = seq // d
            per_res = blocks_per_super // d

            def block(u, c, p=p, d=d, sub_len=sub_len, per_res=per_res):
                r = u % d
                jl = u // d
                jb = sb * per_res + jl
                q0 = jb * ATTN_QBLK
                k0 = jnp.clip(q0 - ATTN_HALF, 0, sub_len - ATTN_KWIN)
                if d == 1:
                    qs, ks = pl.ds(q0, ATTN_QBLK), pl.ds(k0, ATTN_KWIN)
                    os = pl.ds(jl * ATTN_QBLK, ATTN_QBLK)
                else:
                    qs = pl.ds(r + d * q0, ATTN_QBLK, stride=d)
                    ks = pl.ds(r + d * k0, ATTN_KWIN, stride=d)
                    os = pl.ds(r + d * jl * ATTN_QBLK, ATTN_QBLK, stride=d)
                q = q_ref[qs, :].astype(BF16)
                k = k_ref[ks, :].astype(BF16)
                v = v_ref[ks, :].astype(BF16)
                s = lax.dot_general(q, k, (((1,), (1,)), ((), ())), preferred_element_type=F32)
                s = s + bias_ref[(q0 - k0) // ATTN_HALF]
                m = jnp.max(s, axis=-1, keepdims=True)
                e = jnp.exp2((s - m) * exp2_scale)
                den = jnp.sum(e, axis=-1, keepdims=True)
                out = jnp.dot(e.astype(BF16), v, preferred_element_type=F32) / den
                po_ref[p, os, :] = out
                pl_ref[p, os, :] = jnp.broadcast_to(m * scale + jnp.log(den), (ATTN_QBLK, ATTN_HEAD_DIM))
                return c

            lax.fori_loop(0, blocks_per_super, block, 0, unroll=ATTN_UNROLL)

        def merge(i, c):
            rows = pl.ds(pl.multiple_of(i * ATTN_QBLK, ATTN_QBLK), ATTN_QBLK)
            lse = [pl_ref[p, rows, :] for p in range(len(DILATIONS))]
            top = functools.reduce(jnp.maximum, lse)
            w = [jnp.exp(l - top) for l in lse]
            tot = functools.reduce(jnp.add, w)
            acc = functools.reduce(jnp.add, [(w[p] / tot) * po_ref[p, rows, :] for p in range(len(DILATIONS))])
            o_ref[pl.ds(base + i * ATTN_QBLK, ATTN_QBLK), :] = acc.astype(o_ref.dtype)
            return c

        lax.fori_loop(0, blocks_per_super, merge, 0)
        return carry

    lax.fori_loop(0, seq // ATTN_SUPER, super_block, 0)


def _attn(p3, q_gain, k_gain, tables):
    B, S, _ = p3.shape
    assert S % ATTN_SUPER == 0 and S // DILATIONS[-1] >= ATTN_KWIN
    blk = (None, S, ATTN_HEAD_DIM)
    const = lambda b, h: (0, 0)
    once = dict(pipeline_mode=pl.Buffered(1))
    vmem = 2 * 3 * S * 128 * 4 + 2 * S * 128 * 2 + 4 * S * 128 * 4 + 2 * len(DILATIONS) * ATTN_SUPER * 128 * 4 \
        + 4 * MIB
    return pl.pallas_call(
        functools.partial(_attn_kernel, seq=S),
        grid=(B, ATTN_HEADS),
        in_specs=[
            pl.BlockSpec(blk, lambda b, h: (b, 0, h)),
            pl.BlockSpec(blk, lambda b, h: (b, 0, ATTN_HEADS + h)),
            pl.BlockSpec(blk, lambda b, h: (b, 0, 2 * ATTN_HEADS + h)),
            pl.BlockSpec((1, ATTN_HEAD_DIM), const),
            pl.BlockSpec((1, ATTN_HEAD_DIM), const),
            pl.BlockSpec((S, ATTN_HEAD_DIM), const, **once),
            pl.BlockSpec((S, ATTN_HEAD_DIM), const, **once),
        ],
        out_specs=pl.BlockSpec(blk, lambda b, h: (b, 0, h)),
        out_shape=jax.ShapeDtypeStruct((B, S, ATTN_WIDTH), BF16),
        scratch_shapes=[pltpu.VMEM((S, ATTN_HEAD_DIM), F32),
                        pltpu.VMEM((S, ATTN_HEAD_DIM), F32),
                        pltpu.VMEM((len(DILATIONS), ATTN_SUPER, ATTN_HEAD_DIM), F32),
                        pltpu.VMEM((len(DILATIONS), ATTN_SUPER, ATTN_HEAD_DIM), F32),
                        pltpu.VMEM((3, ATTN_QBLK, ATTN_KWIN), F32)],
        compiler_params=_params(("parallel", "parallel"), vmem, claim_all=True),
        name="attn",
    )(p3, p3, p3, q_gain, k_gain, *tables)


def _decay_tables(lg, backward):
    C = RET_CHUNK
    i = lax.broadcasted_iota(jnp.int32, (C, C), 0)
    j = lax.broadcasted_iota(jnp.int32, (C, C), 1)
    diff = (j - i) if backward else (i - j)
    mask = jnp.where(diff >= 0, jnp.exp(lg * jnp.maximum(diff, 0).astype(F32)), 0.0)
    idx = lax.broadcasted_iota(jnp.int32, (C, 1), 0).astype(F32)
    if backward:
        k_pow, q_pow = idx, C - idx
    else:
        k_pow, q_pow = C - 1 - idx, idx + 1
    chunk = jnp.exp(jnp.full((1, RET_DIM), C, F32) * lg)
    return mask, jnp.exp(lg * k_pow), jnp.exp(lg * q_pow), chunk


def _ret_cross(q, k, vb, state, k_scale, q_scale, chunk_decay):
    cross = jnp.dot((q * q_scale).astype(BF16), state.astype(BF16), preferred_element_type=F32)
    kv = lax.dot_general((k * k_scale).astype(BF16), vb, (((0,), (0,)), ((), ())), preferred_element_type=F32)
    return cross, chunk_decay * state + kv


def _ret_bwd_kernel(ld_ref, q_ref, k_ref, v_ref, o_ref, state_ref):
    h, t = pl.program_id(1), pl.program_id(2)

    @pl.when(t == 0)
    def _():
        state_ref[...] = jnp.zeros_like(state_ref)

    _, k_scale, q_scale, chunk_decay = _decay_tables(ld_ref[h], backward=True)
    state = state_ref[...]
    for c in reversed(range(RET_ROWS // RET_CHUNK)):
        rows = slice(c * RET_CHUNK, (c + 1) * RET_CHUNK)
        cross, state = _ret_cross(q_ref[rows, :], k_ref[rows, :], v_ref[rows, :].astype(BF16), state,
                                  k_scale, q_scale, chunk_decay)
        o_ref[rows, :] = cross
    state_ref[...] = state


def _ret_fwd_kernel(ldf_ref, ldb_ref, q_ref, k_ref, v_ref, gate_ref, bwd_ref, gain_ref, o_ref, state_ref):
    h, t = pl.program_id(1), pl.program_id(2)

    @pl.when(t == 0)
    def _():
        state_ref[...] = jnp.zeros_like(state_ref)

    mask_f, k_scale, q_scale, chunk_decay = _decay_tables(ldf_ref[h], backward=False)
    mask = mask_f + _decay_tables(ldb_ref[h], backward=True)[0]
    state = state_ref[...]
    for c in range(RET_ROWS // RET_CHUNK):
        rows = slice(c * RET_CHUNK, (c + 1) * RET_CHUNK)
        q, k, vb = q_ref[rows, :], k_ref[rows, :], v_ref[rows, :].astype(BF16)
        s = lax.dot_general(q.astype(BF16), k.astype(BF16), (((1,), (1,)), ((), ())), preferred_element_type=F32)
        inner = jnp.dot((s * mask).astype(BF16), vb, preferred_element_type=F32)
        cross, state = _ret_cross(q, k, vb, state, k_scale, q_scale, chunk_decay)
        both = _rms_norm_rows(inner + cross + bwd_ref[rows, :], gain_ref[...])
        o_ref[rows, :] = (jax.nn.silu(gate_ref[rows, :]) * both).astype(o_ref.dtype)
    state_ref[...] = state


def _ret_specs(n_t, reverse):
    tmap = (lambda t: n_t - 1 - t) if reverse else (lambda t: t)
    blk = (None, RET_ROWS, RET_DIM)
    seg0 = 3 * ATTN_WIDTH // RET_DIM

    def col(seg):
        return pl.BlockSpec(blk, lambda b, h, t: (b, tmap(t), seg0 + seg * RET_HEADS + h))

    return blk, tmap, col


def _ret_bwd(p3, log_decay):
    B, S, _ = p3.shape
    n_t = S // RET_ROWS
    blk, tmap, col = _ret_specs(n_t, reverse=True)
    vmem = 2 * 4 * RET_ROWS * RET_DIM * 4 + RET_DIM * RET_DIM * 4 + 8 * MIB
    return pl.pallas_call(
        _ret_bwd_kernel,
        grid=(B, RET_HEADS, n_t),
        in_specs=[pl.BlockSpec(memory_space=pltpu.SMEM), col(0), col(1), col(2)],
        out_specs=pl.BlockSpec(blk, lambda b, h, t: (b, tmap(t), h)),
        out_shape=jax.ShapeDtypeStruct((B, S, RET_WIDTH), F32),
        scratch_shapes=[pltpu.VMEM((RET_DIM, RET_DIM), F32)],
        compiler_params=_params(("parallel", "parallel", "arbitrary"), vmem, claim_all=True),
        name="ret_bwd",
    )(log_decay, p3, p3, p3)


def _ret_fwd(p3, bwd, log_decay_fwd, log_decay_bwd, out_gain):
    B, S, _ = p3.shape
    n_t = S // RET_ROWS
    blk, tmap, col = _ret_specs(n_t, reverse=False)
    vmem = 2 * 6 * RET_ROWS * RET_DIM * 4 + RET_DIM * RET_DIM * 4 + 8 * MIB
    smem = pl.BlockSpec(memory_space=pltpu.SMEM)
    return pl.pallas_call(
        _ret_fwd_kernel,
        grid=(B, RET_HEADS, n_t),
        in_specs=[smem, smem, col(0), col(1), col(2), col(3),
                  pl.BlockSpec(blk, lambda b, h, t: (b, t, h)),
                  pl.BlockSpec((None, 1, RET_DIM), lambda b, h, t: (h, 0, 0))],
        out_specs=pl.BlockSpec(blk, lambda b, h, t: (b, t, h)),
        out_shape=jax.ShapeDtypeStruct((B, S, RET_WIDTH), BF16),
        scratch_shapes=[pltpu.VMEM((RET_DIM, RET_DIM), F32)],
        compiler_params=_params(("parallel", "parallel", "arbitrary"), vmem, claim_all=True),
        name="ret_fwd",
    )(log_decay_fwd, log_decay_bwd, p3, p3, p3, p3, bwd, out_gain)


def _mix_out_kernel(x_ref, a_ref, r_ref, w_ref, o_ref):
    acc = jnp.dot(a_ref[...], w_ref[:ATTN_WIDTH, :], preferred_element_type=F32)
    acc += jnp.dot(r_ref[...], w_ref[ATTN_WIDTH:, :], preferred_element_type=F32)
    o_ref[...] = x_ref[...] + acc


def _mix_out(x, a, r, w_out):
    T, D = x.shape
    tm = ROW_TILE
    vmem = 2 * (2 * tm * D * 4 + 2 * tm * SEG * 2 + D * D * 2) + tm * D * 4 + 4 * MIB
    return pl.pallas_call(
        _mix_out_kernel,
        grid=(T // tm,),
        in_specs=[
            pl.BlockSpec((tm, D), lambda i: (i, 0)),
            pl.BlockSpec((tm, ATTN_WIDTH), lambda i: (i, 0)),
            pl.BlockSpec((tm, RET_WIDTH), lambda i: (i, 0)),
            pl.BlockSpec((D, D), lambda i: (0, 0)),
        ],
        out_specs=pl.BlockSpec((tm, D), lambda i: (i, 0)),
        out_shape=jax.ShapeDtypeStruct((T, D), F32),
        compiler_params=_params(("parallel",), vmem, claim_all=True),
        name="mix_out",
    )(x, a, r, w_out)


def _trunk(x3, layers):
    B, S, D = x3.shape
    T = B * S
    x = x3.reshape(T, D)
    attn_tables, ret_tables = _rope_tables(S, ATTN_HEAD_DIM), _rope_tables(S, RET_DIM)
    for lw in layers:
        x = _ffn(x, lw["ffn1_norm"], lw["ffn1_wg"], lw["ffn1_wu"], lw["ffn1_wd"])
        proj = _proj(x, lw["mix_norm"], lw["w_in"], ret_tables, S)
        p3 = proj.reshape(B, S, IN_COLS)
        a = _attn(p3, lw["q_gain"], lw["k_gain"], attn_tables)
        bwd = _ret_bwd(p3, lw["ld_bwd"])
        r = _ret_fwd(p3, bwd, lw["ld_fwd"], lw["ld_bwd"], lw["ret_gain"])
        x = _mix_out(x, a.reshape(T, ATTN_WIDTH), r.reshape(T, RET_WIDTH), lw["w_out"])
        x = _ffn(x, lw["ffn2_norm"], lw["ffn2_wg"], lw["ffn2_wu"], lw["ffn2_wd"])
    return x.reshape(B, S, D)


def kernel(x_prompt, x_sample, ffn1_norm, ffn1_w_gate, ffn1_w_up, ffn1_w_down, mix_norm, w_in, attn_q_norm, attn_k_norm, ret_log_decay_fwd, ret_log_decay_bwd, ret_out_norm, w_out, ffn2_norm, ffn2_w_gate, ffn2_w_up, ffn2_w_down):
    depth = w_in.shape[0]
    layers = []
    for l in range(depth):
        layers.append({
            "ffn1_norm": ffn1_norm[l][None, :].astype(F32),
            "ffn1_wg": _cast_layer(ffn1_w_gate, l, D_MODEL, D_FF_PAD),
            "ffn1_wu": _cast_layer(ffn1_w_up, l, D_MODEL, D_FF_PAD),
            "ffn1_wd": _cast_layer(ffn1_w_down, l, D_FF_PAD, D_MODEL),
            "mix_norm": mix_norm[l][None, :].astype(F32),
            "w_in": _cast_layer(w_in, l, D_MODEL, IN_COLS),
            "q_gain": attn_q_norm[l][None, :].astype(F32),
            "k_gain": attn_k_norm[l][None, :].astype(F32),
            "ld_fwd": ret_log_decay_fwd[l].astype(F32),
            "ld_bwd": ret_log_decay_bwd[l].astype(F32),
            "ret_gain": ret_out_norm[l][:, None, :].astype(F32),
            "w_out": _cast_layer(w_out, l, D_MODEL, D_MODEL),
            "ffn2_norm": ffn2_norm[l][None, :].astype(F32),
            "ffn2_wg": _cast_layer(ffn2_w_gate, l, D_MODEL, D_FF_PAD),
            "ffn2_wu": _cast_layer(ffn2_w_up, l, D_MODEL, D_FF_PAD),
            "ffn2_wd": _cast_layer(ffn2_w_down, l, D_FF_PAD, D_MODEL),
        })
    return (_trunk(x_prompt, layers), _trunk(x_sample, layers))
```

```python
import functools

import jax
import jax.numpy as jnp
from jax import lax
from jax.experimental import pallas as pl
from jax.experimental.pallas import tpu as pltpu

F32 = jnp.float32
BF16 = jnp.bfloat16

D_MODEL = 2048
ATTN_HEAD_DIM = 128
ATTN_HEADS = 8
ATTN_WIDTH = ATTN_HEADS * ATTN_HEAD_DIM
DILATIONS = (1, 4, 16)
ATTN_HALF = 64
RET_HEADS = 4
RET_DIM = 256
RET_WIDTH = RET_HEADS * RET_DIM
RET_CHUNK = 128
D_FF = 5504
IN_COLS = 3 * ATTN_WIDTH + 4 * RET_WIDTH
ROPE_THETA = 10000.0
NORM_EPS = 1e-6
NEG_INF = -1e30
LOG2_E = 1.4426950408889634

V7X_LANES = 128
V7X_SCOPED_VMEM_BYTES = 56 * 1024 * 1024
FF_TILE = 512
D_FF_PAD = -(-D_FF // FF_TILE) * FF_TILE
ROW_TILE = 512
FFN_ROWS = 1024
PROJ_ROWS = 1024
CAST_TILE_ELEMS = 1024 * 1024
SEG = 1024
PROJ_CHUNK = 256
ATTN_QBLK = 128
ATTN_KWIN = 256
ATTN_SUPER = ATTN_QBLK * DILATIONS[-1]
ATTN_UNROLL = 16
RET_ROWS = 1024
MIB = 1024 * 1024


def _params(semantics, vmem_bytes, claim_all=False):
    if claim_all:
        assert vmem_bytes <= V7X_SCOPED_VMEM_BYTES
        vmem_bytes = V7X_SCOPED_VMEM_BYTES
    return pltpu.CompilerParams(dimension_semantics=semantics, vmem_limit_bytes=int(vmem_bytes))


def _rms_norm_rows(x, gain):
    return x * lax.rsqrt(jnp.mean(x * x, axis=-1, keepdims=True) + NORM_EPS) * gain


def _cast_kernel(w_ref, o_ref, *, rows, cols):
    tr, tc = o_ref.shape
    w = w_ref[...]
    if rows % tr or cols % tc:
        r = lax.broadcasted_iota(jnp.int32, (tr, tc), 0) + pl.program_id(0) * tr
        c = lax.broadcasted_iota(jnp.int32, (tr, tc), 1) + pl.program_id(1) * tc
        w = jnp.where((r < rows) & (c < cols), w, 0.0)
    o_ref[...] = w.astype(o_ref.dtype)


def _cast_layer(w, layer, rows_pad, cols_pad):
    _, rows, cols = w.shape
    tc = next(c for c in (2048, 1024, 512) if cols_pad % c == 0)
    tr = CAST_TILE_ELEMS // tc
    assert rows_pad % tr == 0
    return pl.pallas_call(
        functools.partial(_cast_kernel, rows=rows, cols=cols),
        grid=(rows_pad // tr, cols_pad // tc),
        in_specs=[pl.BlockSpec((None, tr, tc), lambda i, j: (layer, i, j))],
        out_specs=pl.BlockSpec((tr, tc), lambda i, j: (i, j)),
        out_shape=jax.ShapeDtypeStruct((rows_pad, cols_pad), BF16),
        compiler_params=_params(("parallel", "parallel"), 2 * tr * tc * (4 + 2) + 4 * MIB, claim_all=True),
        name="cast",
    )(w)


def _ffn_kernel(x_ref, g_ref, wg_ref, wu_ref, wd_ref, o_ref, xn_ref):
    f = pl.program_id(1)
    last = pl.num_programs(1) - 1
    tm = x_ref.shape[0]

    def step(rows, first, final):
        if first:
            xn_ref[rows, :] = _rms_norm_rows(x_ref[rows, :], g_ref[...]).astype(BF16)
        xn = xn_ref[rows, :]
        gate = jnp.dot(xn, wg_ref[...], preferred_element_type=F32)
        up = jnp.dot(xn, wu_ref[...], preferred_element_type=F32)
        h = (jax.nn.silu(gate) * up).astype(BF16)
        acc = jnp.dot(h, wd_ref[...], preferred_element_type=F32)
        if not first:
            acc = o_ref[rows, :] + acc
        o_ref[rows, :] = x_ref[rows, :] + 0.5 * acc if final else acc

    halves = (slice(0, tm // 2), slice(tm // 2, tm))

    @pl.when(f == 0)
    def _():
        for rows in halves:
            step(rows, True, False)

    @pl.when((f > 0) & (f < last))
    def _():
        step(slice(0, tm), False, False)

    @pl.when(f == last)
    def _():
        for rows in halves:
            step(rows, False, True)


def _ffn(x, gain, wg, wu, wd):
    T, D = x.shape
    tm, tf = FFN_ROWS, FF_TILE
    vmem = 2 * (2 * tm * D * 4) + tm * D * 2 + 2 * 3 * D * tf * 2 + 3 * tm * tf * 4 + 2 * MIB
    return pl.pallas_call(
        _ffn_kernel,
        grid=(T // tm, D_FF_PAD // tf),
        in_specs=[
            pl.BlockSpec((tm, D), lambda i, f: (i, 0)),
            pl.BlockSpec((1, D), lambda i, f: (0, 0)),
            pl.BlockSpec((D, tf), lambda i, f: (0, f)),
            pl.BlockSpec((D, tf), lambda i, f: (0, f)),
            pl.BlockSpec((tf, D), lambda i, f: (f, 0)),
        ],
        out_specs=pl.BlockSpec((tm, D), lambda i, f: (i, 0)),
        out_shape=jax.ShapeDtypeStruct((T, D), F32),
        scratch_shapes=[pltpu.VMEM((tm, D), BF16)],
        compiler_params=_params(("parallel", "arbitrary"), vmem),
        name="ffn",
    )(x, gain, wg, wu, wd)


def _proj_kernel(x_ref, g_ref, w_ref, cr_ref, sr_ref, o_ref, xn_ref):
    n = pl.program_id(1)

    tm = x_ref.shape[0]

    def by_chunks(epilogue, rows=slice(None)):
        xn = xn_ref[rows, :]
        for j in range(SEG // PROJ_CHUNK):
            c0 = j * PROJ_CHUNK
            y = jnp.dot(xn, w_ref[:, c0:c0 + PROJ_CHUNK], preferred_element_type=F32)
            epilogue(y, c0, rows)

    def ret_qk(scale):
        def epilogue(y, c0, rows):
            cos, sin = cr_ref[rows, :], sr_ref[rows, :]
            half = RET_DIM // 2
            lo, hi = y[:, :half], y[:, half:]
            o_ref[rows, c0:c0 + half] = (lo * cos[:, :half] + hi * sin[:, :half]) * scale
            o_ref[rows, c0 + half:c0 + RET_DIM] = (hi * cos[:, half:] + lo * sin[:, half:]) * scale
        by_chunks(epilogue)

    def plain(rows=slice(None)):
        def epilogue(y, c0, rows):
            o_ref[rows, c0:c0 + PROJ_CHUNK] = y
        by_chunks(epilogue, rows)

    @pl.when(n == 0)
    def _():
        for rows in (slice(0, tm // 2), slice(tm // 2, tm)):
            xn_ref[rows, :] = _rms_norm_rows(x_ref[rows, :], g_ref[...]).astype(BF16)
            plain(rows)

    @pl.when(n == 3)
    def _():
        ret_qk(1.0)

    @pl.when(n == 4)
    def _():
        ret_qk(RET_DIM ** -0.5)

    @pl.when(((n > 0) & (n < 3)) | (n >= 5))
    def _():
        plain()


def _rope_tables(seq, head_dim):
    half = head_dim // 2
    inv_freq = ROPE_THETA ** (-jnp.arange(half, dtype=F32) / half)
    ang = jnp.arange(seq, dtype=F32)[:, None] * inv_freq[None, :]
    cos, sin = jnp.cos(ang), jnp.sin(ang)
    return jnp.concatenate([cos, cos], axis=-1), jnp.concatenate([-sin, sin], axis=-1)


def _proj(x, gain, w_in, ret_tables, seq):
    T, D = x.shape
    tm = PROJ_ROWS
    nseq = seq // tm
    cr, sr = ret_tables
    pos = lambda i, n: (i % nseq, 0)
    const = lambda i, n: (0, 0)
    vmem = 2 * tm * D * 4 + tm * D * 2 + 2 * D * SEG * 2 + 2 * tm * SEG * 4 + 2 * 2 * tm * RET_DIM * 4 \
        + 6 * tm * PROJ_CHUNK * 4 + 4 * MIB
    return pl.pallas_call(
        _proj_kernel,
        grid=(T // tm, IN_COLS // SEG),
        in_specs=[
            pl.BlockSpec((tm, D), lambda i, n: (i, 0)),
            pl.BlockSpec((1, D), const),
            pl.BlockSpec((D, SEG), lambda i, n: (0, n)),
            pl.BlockSpec((tm, RET_DIM), pos),
            pl.BlockSpec((tm, RET_DIM), pos),
        ],
        out_specs=pl.BlockSpec((tm, SEG), lambda i, n: (i, n)),
        out_shape=jax.ShapeDtypeStruct((T, IN_COLS), F32),
        scratch_shapes=[pltpu.VMEM((tm, D), BF16)],
        compiler_params=_params(("parallel", "arbitrary"), vmem),
        name="proj",
    )(x, gain, w_in, cr, sr)


def _attn_kernel(qraw_ref, kraw_ref, v_ref, qg_ref, kg_ref, cos_ref, sin_ref, o_ref,
                 q_ref, k_ref, po_ref, pl_ref, bias_ref, *, seq):
    scale = ATTN_HEAD_DIM ** -0.5
    exp2_scale = scale * LOG2_E
    blocks_per_super = ATTN_SUPER // ATTN_QBLK

    rel = (lax.broadcasted_iota(jnp.int32, (ATTN_QBLK, ATTN_KWIN), 0)
           - lax.broadcasted_iota(jnp.int32, (ATTN_QBLK, ATTN_KWIN), 1))
    for case in range(3):
        bias_ref[case] = jnp.where(jnp.abs(rel + case * ATTN_HALF) <= ATTN_HALF, 0.0, NEG_INF)

    ones = jnp.ones((ATTN_HEAD_DIM, ATTN_HEAD_DIM), BF16)

    def row_sums(t):
        hi = t.astype(BF16)
        lo = (t - hi.astype(F32)).astype(BF16)
        return jnp.dot(hi, ones, preferred_element_type=F32) + jnp.dot(lo, ones, preferred_element_type=F32)

    def qk_norm_rope(i, c):
        rows = pl.ds(pl.multiple_of(i * ATTN_KWIN, ATTN_KWIN), ATTN_KWIN)
        cos, sin = cos_ref[rows, :], sin_ref[rows, :]
        for raw_ref, gain_ref, dst_ref in ((qraw_ref, qg_ref, q_ref), (kraw_ref, kg_ref, k_ref)):
            y = raw_ref[rows, :]
            mean_sq = row_sums(y * y) * (1.0 / ATTN_HEAD_DIM)
            yn = y * lax.rsqrt(mean_sq + NORM_EPS) * gain_ref[...]
            dst_ref[rows, :] = yn * cos + pltpu.roll(yn, ATTN_HEAD_DIM // 2, 1) * sin
        return c

    lax.fori_loop(0, seq // ATTN_KWIN, qk_norm_rope, 0, unroll=4)

    def super_block(sb, carry):
        base = pl.multiple_of(sb * ATTN_SUPER, ATTN_SUPER)
        for p, d in enumerate(DILATIONS):
            sub_len = seq // d
            per_res = blocks_per_super // d

            def block(u, c, p=p, d=d, sub_len=sub_len, per_res=per_res):
                r = u % d
                jl = u // d
                jb = sb * per_res + jl
                q0 = jb * ATTN_QBLK
                k0 = jnp.clip(q0 - ATTN_HALF, 0, sub_len - ATTN_KWIN)
                if d == 1:
                    qs, ks = pl.ds(q0, ATTN_QBLK), pl.ds(k0, ATTN_KWIN)
                    os = pl.ds(jl * ATTN_QBLK, ATTN_QBLK)
                else:
                    qs = pl.ds(r + d * q0, ATTN_QBLK, stride=d)
                    ks = pl.ds(r + d * k0, ATTN_KWIN, stride=d)
                    os = pl.ds(r + d * jl * ATTN_QBLK, ATTN_QBLK, stride=d)
                q = q_ref[qs, :].astype(BF16)
                k = k_ref[ks, :].astype(BF16)
                v = v_ref[ks, :].astype(BF16)
                s = lax.dot_general(q, k, (((1,), (1,)), ((), ())), preferred_element_type=F32)
                s = s + bias_ref[(q0 - k0) // ATTN_HALF]
                m = jnp.max(s, axis=-1, keepdims=True)
                e = jnp.exp2((s - m) * exp2_scale)
                den = jnp.sum(e, axis=-1, keepdims=True)
                out = jnp.dot(e.astype(BF16), v, preferred_element_type=F32) / den
                po_ref[p, os, :] = out
                pl_ref[p, os, :] = jnp.broadcast_to(m * exp2_scale + jnp.log2(den), (ATTN_QBLK, ATTN_HEAD_DIM))
                return c

            lax.fori_loop(0, blocks_per_super, block, 0, unroll=ATTN_UNROLL)

        def merge(i, c):
            rows = pl.ds(pl.multiple_of(i * ATTN_QBLK, ATTN_QBLK), ATTN_QBLK)
            lse = [pl_ref[p, rows, :] for p in range(len(DILATIONS))]
            top = functools.reduce(jnp.maximum, lse)
            w = [jnp.exp2(l - top) for l in lse]
            tot = functools.reduce(jnp.add, w)
            acc = functools.reduce(jnp.add, [w[p] * po_ref[p, rows, :] for p in range(len(DILATIONS))])
            o_ref[pl.ds(base + i * ATTN_QBLK, ATTN_QBLK), :] = (acc / tot).astype(o_ref.dtype)
            return c

        lax.fori_loop(0, blocks_per_super, merge, 0)
        return carry

    lax.fori_loop(0, seq // ATTN_SUPER, super_block, 0)


def _attn(p3, q_gain, k_gain, tables):
    B, S, _ = p3.shape
    assert S % ATTN_SUPER == 0 and S // DILATIONS[-1] >= ATTN_KWIN
    blk = (None, S, ATTN_HEAD_DIM)
    const = lambda b, h: (0, 0)
    once = dict(pipeline_mode=pl.Buffered(1))
    vmem = 2 * 3 * S * 128 * 4 + 2 * S * 128 * 2 + 4 * S * 128 * 4 + 2 * len(DILATIONS) * ATTN_SUPER * 128 * 4 \
        + 4 * MIB
    return pl.pallas_call(
        functools.partial(_attn_kernel, seq=S),
        grid=(B, ATTN_HEADS),
        in_specs=[
            pl.BlockSpec(blk, lambda b, h: (b, 0, h)),
            pl.BlockSpec(blk, lambda b, h: (b, 0, ATTN_HEADS + h)),
            pl.BlockSpec(blk, lambda b, h: (b, 0, 2 * ATTN_HEADS + h)),
            pl.BlockSpec((1, ATTN_HEAD_DIM), const),
            pl.BlockSpec((1, ATTN_HEAD_DIM), const),
            pl.BlockSpec((S, ATTN_HEAD_DIM), const, **once),
            pl.BlockSpec((S, ATTN_HEAD_DIM), const, **once),
        ],
        out_specs=pl.BlockSpec(blk, lambda b, h: (b, 0, h)),
        out_shape=jax.ShapeDtypeStruct((B, S, ATTN_WIDTH), BF16),
        scratch_shapes=[pltpu.VMEM((S, ATTN_HEAD_DIM), F32),
                        pltpu.VMEM((S, ATTN_HEAD_DIM), F32),
                        pltpu.VMEM((len(DILATIONS), ATTN_SUPER, ATTN_HEAD_DIM), F32),
                        pltpu.VMEM((len(DILATIONS), ATTN_SUPER, ATTN_HEAD_DIM), F32),
                        pltpu.VMEM((3, ATTN_QBLK, ATTN_KWIN), F32)],
        compiler_params=_params(("parallel", "parallel"), vmem, claim_all=True),
        name="attn",
    )(p3, p3, p3, q_gain, k_gain, *tables)


def _decay_tables(lg, backward):
    C = RET_CHUNK
    i = lax.broadcasted_iota(jnp.int32, (C, C), 0)
    j = lax.broadcasted_iota(jnp.int32, (C, C), 1)
    diff = (j - i) if backward else (i - j)
    mask = jnp.where(diff >= 0, jnp.exp(lg * jnp.maximum(diff, 0).astype(F32)), 0.0)
    idx = lax.broadcasted_iota(jnp.int32, (C, 1), 0).astype(F32)
    if backward:
        k_pow, q_pow = idx, C - idx
    else:
        k_pow, q_pow = C - 1 - idx, idx + 1
    chunk = jnp.exp(jnp.full((1, RET_DIM), C, F32) * lg)
    return mask, jnp.exp(lg * k_pow), jnp.exp(lg * q_pow), chunk


def _ret_cross(q, k, vb, state, k_scale, q_scale, chunk_decay):
    cross = jnp.dot((q * q_scale).astype(BF16), state.astype(BF16), preferred_element_type=F32)
    kv = lax.dot_general((k * k_scale).astype(BF16), vb, (((0,), (0,)), ((), ())), preferred_element_type=F32)
    return cross, chunk_decay * state + kv


def _ret_bwd_rows(lg, q_ref, k_ref, v_ref, o_ref, state_ref, row0):
    _, k_scale, q_scale, chunk_decay = _decay_tables(lg, backward=True)
    state = state_ref[...]
    for c in reversed(range(RET_ROWS // RET_CHUNK)):
        rows = slice(row0 + c * RET_CHUNK, row0 + (c + 1) * RET_CHUNK)
        cross, state = _ret_cross(q_ref[rows, :], k_ref[rows, :], v_ref[rows, :].astype(BF16), state,
                                  k_scale, q_scale, chunk_decay)
        o_ref[rows, :] = cross
    state_ref[...] = state


def _ret_fwd_rows(lg_fwd, lg_bwd, q_ref, k_ref, v_ref, gate_ref, bwd_ref, gain_ref, o_ref, state_ref, row0):
    mask_f, k_scale, q_scale, chunk_decay = _decay_tables(lg_fwd, backward=False)
    mask = mask_f + _decay_tables(lg_bwd, backward=True)[0]
    state = state_ref[...]
    for c in range(RET_ROWS // RET_CHUNK):
        rows = slice(row0 + c * RET_CHUNK, row0 + (c + 1) * RET_CHUNK)
        q, k, vb = q_ref[rows, :], k_ref[rows, :], v_ref[rows, :].astype(BF16)
        s = lax.dot_general(q.astype(BF16), k.astype(BF16), (((1,), (1,)), ((), ())), preferred_element_type=F32)
        inner = jnp.dot((s * mask).astype(BF16), vb, preferred_element_type=F32)
        cross, state = _ret_cross(q, k, vb, state, k_scale, q_scale, chunk_decay)
        both = _rms_norm_rows(inner + cross + bwd_ref[rows, :], gain_ref[...])
        o_ref[rows, :] = (jax.nn.silu(gate_ref[rows, :]) * both).astype(o_ref.dtype)
    state_ref[...] = state


def _ret_bwd_kernel(ld_ref, q_ref, k_ref, v_ref, o_ref, state_ref):
    h, t = pl.program_id(1), pl.program_id(2)

    @pl.when(t == 0)
    def _():
        state_ref[...] = jnp.zeros_like(state_ref)

    _ret_bwd_rows(ld_ref[h], q_ref, k_ref, v_ref, o_ref, state_ref, 0)


def _ret_fwd_kernel(ldf_ref, ldb_ref, q_ref, k_ref, v_ref, gate_ref, bwd_ref, gain_ref, o_ref, state_ref):
    h, t = pl.program_id(1), pl.program_id(2)

    @pl.when(t == 0)
    def _():
        state_ref[...] = jnp.zeros_like(state_ref)

    _ret_fwd_rows(ldf_ref[h], ldb_ref[h], q_ref, k_ref, v_ref, gate_ref, bwd_ref, gain_ref, o_ref, state_ref, 0)


def _ret_fused_kernel(ldf_ref, ldb_ref, q_ref, k_ref, v_ref, gate_ref, gain_ref, o_ref, bwd_ref, state_ref, *, seq):
    h = pl.program_id(1)
    blocks = range(0, seq, RET_ROWS)
    state_ref[...] = jnp.zeros_like(state_ref)
    for row0 in reversed(blocks):
        _ret_bwd_rows(ldb_ref[h], q_ref, k_ref, v_ref, bwd_ref, state_ref, row0)
    state_ref[...] = jnp.zeros_like(state_ref)
    for row0 in blocks:
        _ret_fwd_rows(ldf_ref[h], ldb_ref[h], q_ref, k_ref, v_ref, gate_ref, bwd_ref, gain_ref, o_ref, state_ref, row0)


def _ret_specs(n_t, reverse):
    tmap = (lambda t: n_t - 1 - t) if reverse else (lambda t: t)
    blk = (None, RET_ROWS, RET_DIM)
    seg0 = 3 * ATTN_WIDTH // RET_DIM

    def col(seg):
        return pl.BlockSpec(blk, lambda b, h, t: (b, tmap(t), seg0 + seg * RET_HEADS + h))

    return blk, tmap, col


def _ret_bwd(p3, log_decay):
    B, S, _ = p3.shape
    n_t = S // RET_ROWS
    blk, tmap, col = _ret_specs(n_t, reverse=True)
    vmem = 2 * 4 * RET_ROWS * RET_DIM * 4 + RET_DIM * RET_DIM * 4 + 8 * MIB
    return pl.pallas_call(
        _ret_bwd_kernel,
        grid=(B, RET_HEADS, n_t),
        in_specs=[pl.BlockSpec(memory_space=pltpu.SMEM), col(0), col(1), col(2)],
        out_specs=pl.BlockSpec(blk, lambda b, h, t: (b, tmap(t), h)),
        out_shape=jax.ShapeDtypeStruct((B, S, RET_WIDTH), F32),
        scratch_shapes=[pltpu.VMEM((RET_DIM, RET_DIM), F32)],
        compiler_params=_params(("parallel", "parallel", "arbitrary"), vmem, claim_all=True),
        name="ret_bwd",
    )(log_decay, p3, p3, p3)


def _ret_fwd(p3, bwd, log_decay_fwd, log_decay_bwd, out_gain):
    B, S, _ = p3.shape
    n_t = S // RET_ROWS
    blk, tmap, col = _ret_specs(n_t, reverse=False)
    vmem = 2 * 6 * RET_ROWS * RET_DIM * 4 + RET_DIM * RET_DIM * 4 + 8 * MIB
    smem = pl.BlockSpec(memory_space=pltpu.SMEM)
    return pl.pallas_call(
        _ret_fwd_kernel,
        grid=(B, RET_HEADS, n_t),
        in_specs=[smem, smem, col(0), col(1), col(2), col(3),
                  pl.BlockSpec(blk, lambda b, h, t: (b, t, h)),
                  pl.BlockSpec((None, 1, RET_DIM), lambda b, h, t: (h, 0, 0))],
        out_specs=pl.BlockSpec(blk, lambda b, h, t: (b, t, h)),
        out_shape=jax.ShapeDtypeStruct((B, S, RET_WIDTH), BF16),
        scratch_shapes=[pltpu.VMEM((RET_DIM, RET_DIM), F32)],
        compiler_params=_params(("parallel", "parallel", "arbitrary"), vmem, claim_all=True),
        name="ret_fwd",
    )(log_decay_fwd, log_decay_bwd, p3, p3, p3, p3, bwd, out_gain)


def _retention(p3, log_decay_fwd, log_decay_bwd, out_gain):
    B, S, _ = p3.shape
    seq_bytes = S * RET_DIM * 4
    fused_vmem = 2 * 4 * seq_bytes + seq_bytes + 2 * seq_bytes // 2 + RET_DIM * RET_DIM * 4 + 6 * MIB
    if fused_vmem > V7X_SCOPED_VMEM_BYTES:
        bwd = _ret_bwd(p3, log_decay_bwd)
        return _ret_fwd(p3, bwd, log_decay_fwd, log_decay_bwd, out_gain)
    blk = (None, S, RET_DIM)
    seg0 = 3 * ATTN_WIDTH // RET_DIM
    col = lambda seg: pl.BlockSpec(blk, lambda b, h: (b, 0, seg0 + seg * RET_HEADS + h))
    smem = pl.BlockSpec(memory_space=pltpu.SMEM)
    return pl.pallas_call(
        functools.partial(_ret_fused_kernel, seq=S),
        grid=(B, RET_HEADS),
        in_specs=[smem, smem, col(0), col(1), col(2), col(3),
                  pl.BlockSpec((None, 1, RET_DIM), lambda b, h: (h, 0, 0))],
        out_specs=pl.BlockSpec(blk, lambda b, h: (b, 0, h)),
        out_shape=jax.ShapeDtypeStruct((B, S, RET_WIDTH), BF16),
        scratch_shapes=[pltpu.VMEM((S, RET_DIM), F32), pltpu.VMEM((RET_DIM, RET_DIM), F32)],
        compiler_params=_params(("parallel", "parallel"), fused_vmem, claim_all=True),
        name="ret_fused",
    )(log_decay_fwd, log_decay_bwd, p3, p3, p3, p3, out_gain)


def _mix_out_kernel(x_ref, a_ref, r_ref, w_ref, o_ref):
    acc = jnp.dot(a_ref[...], w_ref[:ATTN_WIDTH, :], preferred_element_type=F32)
    acc += jnp.dot(r_ref[...], w_ref[ATTN_WIDTH:, :], preferred_element_type=F32)
    o_ref[...] = x_ref[...] + acc


def _mix_out(x, a, r, w_out):
    T, D = x.shape
    tm = ROW_TILE
    vmem = 2 * (2 * tm * D * 4 + 2 * tm * SEG * 2 + D * D * 2) + tm * D * 4 + 4 * MIB
    return pl.pallas_call(
        _mix_out_kernel,
        grid=(T // tm,),
        in_specs=[
            pl.BlockSpec((tm, D), lambda i: (i, 0)),
            pl.BlockSpec((tm, ATTN_WIDTH), lambda i: (i, 0)),
            pl.BlockSpec((tm, RET_WIDTH), lambda i: (i, 0)),
            pl.BlockSpec((D, D), lambda i: (0, 0)),
        ],
        out_specs=pl.BlockSpec((tm, D), lambda i: (i, 0)),
        out_shape=jax.ShapeDtypeStruct((T, D), F32),
        compiler_params=_params(("parallel",), vmem, claim_all=True),
        name="mix_out",
    )(x, a, r, w_out)


def _trunk(x3, layers):
    B, S, D = x3.shape
    T = B * S
    x = x3.reshape(T, D)
    attn_tables, ret_tables = _rope_tables(S, ATTN_HEAD_DIM), _rope_tables(S, RET_DIM)
    for lw in layers:
        x = _ffn(x, lw["ffn1_norm"], lw["ffn1_wg"], lw["ffn1_wu"], lw["ffn1_wd"])
        proj = _proj(x, lw["mix_norm"], lw["w_in"], ret_tables, S)
        p3 = proj.reshape(B, S, IN_COLS)
        a = _attn(p3, lw["q_gain"], lw["k_gain"], attn_tables)
        r = _retention(p3, lw["ld_fwd"], lw["ld_bwd"], lw["ret_gain"])
        x = _mix_out(x, a.reshape(T, ATTN_WIDTH), r.reshape(T, RET_WIDTH), lw["w_out"])
        x = _ffn(x, lw["ffn2_norm"], lw["ffn2_wg"], lw["ffn2_wu"], lw["ffn2_wd"])
    return x.reshape(B, S, D)


def kernel(x_prompt, x_sample, ffn1_norm, ffn1_w_gate, ffn1_w_up, ffn1_w_down, mix_norm, w_in, attn_q_norm, attn_k_norm, ret_log_decay_fwd, ret_log_decay_bwd, ret_out_norm, w_out, ffn2_norm, ffn2_w_gate, ffn2_w_up, ffn2_w_down):
    depth = w_in.shape[0]
    layers = []
    for l in range(depth):
        layers.append({
            "ffn1_norm": ffn1_norm[l][None, :].astype(F32),
            "ffn1_wg": _cast_layer(ffn1_w_gate, l, D_MODEL, D_FF_PAD),
            "ffn1_wu": _cast_layer(ffn1_w_up, l, D_MODEL, D_FF_PAD),
            "ffn1_wd": _cast_layer(ffn1_w_down, l, D_FF_PAD, D_MODEL),
            "mix_norm": mix_norm[l][None, :].astype(F32),
            "w_in": _cast_layer(w_in, l, D_MODEL, IN_COLS),
            "q_gain": attn_q_norm[l][None, :].astype(F32),
            "k_gain": attn_k_norm[l][None, :].astype(F32),
            "ld_fwd": ret_log_decay_fwd[l].astype(F32),
            "ld_bwd": ret_log_decay_bwd[l].astype(F32),
            "ret_gain": ret_out_norm[l][:, None, :].astype(F32),
            "w_out": _cast_layer(w_out, l, D_MODEL, D_MODEL),
            "ffn2_norm": ffn2_norm[l][None, :].astype(F32),
            "ffn2_wg": _cast_layer(ffn2_w_gate, l, D_MODEL, D_FF_PAD),
            "ffn2_wu": _cast_layer(ffn2_w_up, l, D_MODEL, D_FF_PAD),
            "ffn2_wd": _cast_layer(ffn2_w_down, l, D_FF_PAD, D_MODEL),
        })
    return (_trunk(x_prompt, layers), _trunk(x_sample, layers))
```

```python
import functools
from typing import NamedTuple

import jax
import jax.numpy as jnp
from jax import lax
from jax.experimental import pallas as pl
from jax.experimental.pallas import tpu as pltpu

F32 = jnp.float32
BF16 = jnp.bfloat16

D_MODEL = 2048
ATTN_HEAD_DIM = 128
ATTN_HEADS = 8
ATTN_WIDTH = ATTN_HEADS * ATTN_HEAD_DIM
DILATIONS = (1, 4, 16)
ATTN_HALF = 64
RET_HEADS = 4
RET_DIM = 256
RET_WIDTH = RET_HEADS * RET_DIM
RET_CHUNK = 128
D_FF = 5504
IN_COLS = 3 * ATTN_WIDTH + 4 * RET_WIDTH
ROPE_THETA = 10000.0
NORM_EPS = 1e-6
NEG_INF = -1e30
LOG2_E = 1.4426950408889634

V7X_LANES = 128
V7X_BF16_SUBLANES = 16
V7X_SCOPED_VMEM_BYTES = 56 * 1024 * 1024
FF_TILE = 512
D_FF_PAD = -(-D_FF // FF_TILE) * FF_TILE
ROW_TILE = 512
FFN_ROWS = 1024
PROJ_ROWS = 1024
CAST_TILE_ELEMS = 1024 * 1024
SEG = 1024
PROJ_CHUNK = 256
ATTN_QBLK = 128
ATTN_KWIN = 256
ATTN_SUPER = ATTN_QBLK * DILATIONS[-1]
ATTN_UNROLL = 16
RET_ROWS = 1024
MIB = 1024 * 1024


def _params(semantics, vmem_bytes, claim_all=False):
    if claim_all:
        assert vmem_bytes <= V7X_SCOPED_VMEM_BYTES
        vmem_bytes = V7X_SCOPED_VMEM_BYTES
    return pltpu.CompilerParams(dimension_semantics=semantics, vmem_limit_bytes=int(vmem_bytes))


def _rms_norm_rows(x, gain):
    return x * lax.rsqrt(jnp.mean(x * x, axis=-1, keepdims=True) + NORM_EPS) * gain


def _cast_tile(w_ref, o_ref, row_tile, col_tile, rows, cols):
    tr, tc = o_ref.shape
    w = w_ref[...]
    if rows % tr or cols % tc:
        r = lax.broadcasted_iota(jnp.int32, (tr, tc), 0) + row_tile * tr
        c = lax.broadcasted_iota(jnp.int32, (tr, tc), 1) + col_tile * tc
        w = jnp.where((r < rows) & (c < cols), w, 0.0)
    o_ref[...] = w.astype(o_ref.dtype)


def _cast_kernel(w_ref, o_ref, *, rows, cols):
    _cast_tile(w_ref, o_ref, pl.program_id(0), pl.program_id(1), rows, cols)


class _CarriedCast(NamedTuple):
    w: jax.Array
    layer: int
    rows_pad: int
    cols_pad: int

    def tiling(self, n_row_blocks, n_col_steps):
        tr = self.rows_pad // n_row_blocks
        n_col_tiles = max(n for n in range(1, n_col_steps + 1)
                          if self.cols_pad % n == 0 and (self.cols_pad // n) % V7X_LANES == 0)
        assert self.rows_pad % n_row_blocks == 0 and tr % V7X_BF16_SUBLANES == 0
        return tr, self.cols_pad // n_col_tiles, n_col_tiles

    def specs(self, n_row_blocks, n_col_steps):
        tr, tc, n_col_tiles = self.tiling(n_row_blocks, n_col_steps)
        layer = self.layer
        col = lambda f: jnp.minimum(f, n_col_tiles - 1)
        return (pl.BlockSpec((None, tr, tc), lambda i, f: (layer, i, col(f))),
                pl.BlockSpec((tr, tc), lambda i, f: (i, col(f))),
                jax.ShapeDtypeStruct((self.rows_pad, self.cols_pad), BF16))


def _run_carried_casts(extents, in_refs, out_refs, i, f):
    for (rows, cols, n_col_tiles), w_ref, o_ref in zip(extents, in_refs, out_refs):
        _cast_tile(w_ref, o_ref, i, jnp.minimum(f, n_col_tiles - 1), rows, cols)


def _cast_layer(w, layer, rows_pad, cols_pad):
    _, rows, cols = w.shape
    tc = next(c for c in (2048, 1024, 512) if cols_pad % c == 0)
    tr = CAST_TILE_ELEMS // tc
    assert rows_pad % tr == 0
    return pl.pallas_call(
        functools.partial(_cast_kernel, rows=rows, cols=cols),
        grid=(rows_pad // tr, cols_pad // tc),
        in_specs=[pl.BlockSpec((None, tr, tc), lambda i, j: (layer, i, j))],
        out_specs=pl.BlockSpec((tr, tc), lambda i, j: (i, j)),
        out_shape=jax.ShapeDtypeStruct((rows_pad, cols_pad), BF16),
        compiler_params=_params(("parallel", "parallel"), 2 * tr * tc * (4 + 2) + 4 * MIB, claim_all=True),
        name="cast",
    )(w)


def _ffn_kernel(x_ref, g_ref, wg_ref, wu_ref, wd_ref, *refs, cast_extents):
    n_casts = len(cast_extents)
    cast_in, o_ref, cast_out, xn_ref = refs[:n_casts], refs[n_casts], refs[n_casts + 1:-1], refs[-1]
    i, f = pl.program_id(0), pl.program_id(1)
    last = pl.num_programs(1) - 1
    tm = x_ref.shape[0]

    def step(rows, first, final):
        if first:
            xn_ref[rows, :] = _rms_norm_rows(x_ref[rows, :], g_ref[...]).astype(BF16)
        xn = xn_ref[rows, :]
        gate = jnp.dot(xn, wg_ref[...], preferred_element_type=F32)
        up = jnp.dot(xn, wu_ref[...], preferred_element_type=F32)
        h = (jax.nn.silu(gate) * up).astype(BF16)
        acc = jnp.dot(h, wd_ref[...], preferred_element_type=F32)
        if not first:
            acc = o_ref[rows, :] + acc
        o_ref[rows, :] = x_ref[rows, :] + 0.5 * acc if final else acc

    halves = (slice(0, tm // 2), slice(tm // 2, tm))

    @pl.when(f == 0)
    def _():
        _run_carried_casts(cast_extents, cast_in, cast_out, i, f)
        for rows in halves:
            step(rows, True, False)

    @pl.when((f > 0) & (f < last))
    def _():
        _run_carried_casts(cast_extents, cast_in, cast_out, i, f)
        step(slice(0, tm), False, False)

    @pl.when(f == last)
    def _():
        _run_carried_casts(cast_extents, cast_in, cast_out, i, f)
        for rows in halves:
            step(rows, False, True)


def _ffn(x, gain, wg, wu, wd, casts=()):
    T, D = x.shape
    tm, tf = FFN_ROWS, FF_TILE
    grid = (T // tm, D_FF_PAD // tf)
    cast_specs = [c.specs(*grid) for c in casts]
    cast_extents = tuple(c.w.shape[1:] + (c.tiling(*grid)[2],) for c in casts)
    cast_vmem = sum(2 * tr * tc * (4 + 2) for tr, tc, _ in (c.tiling(*grid) for c in casts))
    vmem = 2 * (2 * tm * D * 4) + tm * D * 2 + 2 * 3 * D * tf * 2 + 3 * tm * tf * 4 + cast_vmem + 2 * MIB
    y, *cast_out = pl.pallas_call(
        functools.partial(_ffn_kernel, cast_extents=cast_extents),
        grid=grid,
        in_specs=[
            pl.BlockSpec((tm, D), lambda i, f: (i, 0)),
            pl.BlockSpec((1, D), lambda i, f: (0, 0)),
            pl.BlockSpec((D, tf), lambda i, f: (0, f)),
            pl.BlockSpec((D, tf), lambda i, f: (0, f)),
            pl.BlockSpec((tf, D), lambda i, f: (f, 0)),
        ] + [s[0] for s in cast_specs],
        out_specs=[pl.BlockSpec((tm, D), lambda i, f: (i, 0))] + [s[1] for s in cast_specs],
        out_shape=[jax.ShapeDtypeStruct((T, D), F32)] + [s[2] for s in cast_specs],
        scratch_shapes=[pltpu.VMEM((tm, D), BF16)],
        compiler_params=_params(("parallel", "arbitrary"), vmem),
        name="ffn",
    )(x, gain, wg, wu, wd, *[c.w for c in casts])
    return y, cast_out


def _proj_kernel(x_ref, g_ref, w_ref, cr_ref, sr_ref, o_ref, xn_ref):
    n = pl.program_id(1)

    tm = x_ref.shape[0]

    def by_chunks(epilogue, rows=slice(None)):
        xn = xn_ref[rows, :]
        for j in range(SEG // PROJ_CHUNK):
            c0 = j * PROJ_CHUNK
            y = jnp.dot(xn, w_ref[:, c0:c0 + PROJ_CHUNK], preferred_element_type=F32)
            epilogue(y, c0, rows)

    def ret_qk(scale):
        def epilogue(y, c0, rows):
            cos, sin = cr_ref[rows, :], sr_ref[rows, :]
            half = RET_DIM // 2
            lo, hi = y[:, :half], y[:, half:]
            o_ref[rows, c0:c0 + half] = (lo * cos[:, :half] + hi * sin[:, :half]) * scale
            o_ref[rows, c0 + half:c0 + RET_DIM] = (hi * cos[:, half:] + lo * sin[:, half:]) * scale
        by_chunks(epilogue)

    def plain(rows=slice(None)):
        def epilogue(y, c0, rows):
            o_ref[rows, c0:c0 + PROJ_CHUNK] = y
        by_chunks(epilogue, rows)

    @pl.when(n == 0)
    def _():
        for rows in (slice(0, tm // 2), slice(tm // 2, tm)):
            xn_ref[rows, :] = _rms_norm_rows(x_ref[rows, :], g_ref[...]).astype(BF16)
            plain(rows)

    @pl.when(n == 3)
    def _():
        ret_qk(1.0)

    @pl.when(n == 4)
    def _():
        ret_qk(RET_DIM ** -0.5)

    @pl.when(((n > 0) & (n < 3)) | (n >= 5))
    def _():
        plain()


def _rope_tables(seq, head_dim):
    half = head_dim // 2
    inv_freq = ROPE_THETA ** (-jnp.arange(half, dtype=F32) / half)
    ang = jnp.arange(seq, dtype=F32)[:, None] * inv_freq[None, :]
    cos, sin = jnp.cos(ang), jnp.sin(ang)
    return jnp.concatenate([cos, cos], axis=-1), jnp.concatenate([-sin, sin], axis=-1)


def _proj(x, gain, w_in, ret_tables, seq):
    T, D = x.shape
    tm = PROJ_ROWS
    nseq = seq // tm
    cr, sr = ret_tables
    pos = lambda i, n: (i % nseq, 0)
    const = lambda i, n: (0, 0)
    vmem = 2 * tm * D * 4 + tm * D * 2 + 2 * D * SEG * 2 + 2 * tm * SEG * 4 + 2 * 2 * tm * RET_DIM * 4 \
        + 6 * tm * PROJ_CHUNK * 4 + 4 * MIB
    return pl.pallas_call(
        _proj_kernel,
        grid=(T // tm, IN_COLS // SEG),
        in_specs=[
            pl.BlockSpec((tm, D), lambda i, n: (i, 0)),
            pl.BlockSpec((1, D), const),
            pl.BlockSpec((D, SEG), lambda i, n: (0, n)),
            pl.BlockSpec((tm, RET_DIM), pos),
            pl.BlockSpec((tm, RET_DIM), pos),
        ],
        out_specs=pl.BlockSpec((tm, SEG), lambda i, n: (i, n)),
        out_shape=jax.ShapeDtypeStruct((T, IN_COLS), F32),
        scratch_shapes=[pltpu.VMEM((tm, D), BF16)],
        compiler_params=_params(("parallel", "arbitrary"), vmem),
        name="proj",
    )(x, gain, w_in, cr, sr)


def _attn_kernel(qraw_ref, kraw_ref, v_ref, qg_ref, kg_ref, cos_ref, sin_ref, o_ref,
                 q_ref, k_ref, po_ref, pl_ref, bias_ref, *, seq):
    scale = ATTN_HEAD_DIM ** -0.5
    exp2_scale = scale * LOG2_E
    blocks_per_super = ATTN_SUPER // ATTN_QBLK

    rel = (lax.broadcasted_iota(jnp.int32, (ATTN_QBLK, ATTN_KWIN), 0)
           - lax.broadcasted_iota(jnp.int32, (ATTN_QBLK, ATTN_KWIN), 1))
    for case in range(3):
        bias_ref[case] = jnp.where(jnp.abs(rel + case * ATTN_HALF) <= ATTN_HALF, 0.0, NEG_INF)

    ones = jnp.ones((ATTN_HEAD_DIM, ATTN_HEAD_DIM), BF16)

    def row_sums(t):
        hi = t.astype(BF16)
        lo = (t - hi.astype(F32)).astype(BF16)
        return jnp.dot(hi, ones, preferred_element_type=F32) + jnp.dot(lo, ones, preferred_element_type=F32)

    def qk_norm_rope(i, c):
        rows = pl.ds(pl.multiple_of(i * ATTN_KWIN, ATTN_KWIN), ATTN_KWIN)
        cos, sin = cos_ref[rows, :], sin_ref[rows, :]
        for raw_ref, gain_ref, dst_ref in ((qraw_ref, qg_ref, q_ref), (kraw_ref, kg_ref, k_ref)):
            y = raw_ref[rows, :]
            mean_sq = row_sums(y * y) * (1.0 / ATTN_HEAD_DIM)
            yn = y * lax.rsqrt(mean_sq + NORM_EPS) * gain_ref[...]
            dst_ref[rows, :] = yn * cos + pltpu.roll(yn, ATTN_HEAD_DIM // 2, 1) * sin
        return c

    lax.fori_loop(0, seq // ATTN_KWIN, qk_norm_rope, 0, unroll=4)

    def super_block(sb, carry):
        base = pl.multiple_of(sb * ATTN_SUPER, ATTN_SUPER)

        def blocks(u, c):
            for p, d in enumerate(DILATIONS):
                sub_len = seq // d
                per_res = blocks_per_super // d
                r = u % d
                jl = u // d
                jb = sb * per_res + jl
                q0 = jb * ATTN_QBLK
                k0 = jnp.clip(q0 - ATTN_HALF, 0, sub_len - ATTN_KWIN)
                if d == 1:
                    qs, ks = pl.ds(q0, ATTN_QBLK), pl.ds(k0, ATTN_KWIN)
                    os = pl.ds(jl * ATTN_QBLK, ATTN_QBLK)
                else:
                    qs = pl.ds(r + d * q0, ATTN_QBLK, stride=d)
                    ks = pl.ds(r + d * k0, ATTN_KWIN, stride=d)
                    os = pl.ds(r + d * jl * ATTN_QBLK, ATTN_QBLK, stride=d)
                q = q_ref[qs, :].astype(BF16)
                k = k_ref[ks, :].astype(BF16)
                v = v_ref[ks, :].astype(BF16)
                s = lax.dot_general(q, k, (((1,), (1,)), ((), ())), preferred_element_type=F32)
                s = s + bias_ref[(q0 - k0) // ATTN_HALF]
                m = jnp.max(s, axis=-1, keepdims=True)
                e = jnp.exp2((s - m) * exp2_scale)
                den = jnp.sum(e, axis=-1, keepdims=True)
                out = jnp.dot(e.astype(BF16), v, preferred_element_type=F32) / den
                po_ref[p, os, :] = out
                pl_ref[p, os, :] = jnp.broadcast_to(m * exp2_scale + jnp.log2(den), (ATTN_QBLK, ATTN_HEAD_DIM))
            return c

        lax.fori_loop(0, blocks_per_super, blocks, 0, unroll=ATTN_UNROLL)

        def merge(i, c):
            rows = pl.ds(pl.multiple_of(i * ATTN_QBLK, ATTN_QBLK), ATTN_QBLK)
            lse = [pl_ref[p, rows, :] for p in range(len(DILATIONS))]
            top = functools.reduce(jnp.maximum, lse)
            w = [jnp.exp2(l - top) for l in lse]
            tot = functools.reduce(jnp.add, w)
            acc = functools.reduce(jnp.add, [w[p] * po_ref[p, rows, :] for p in range(len(DILATIONS))])
            o_ref[pl.ds(base + i * ATTN_QBLK, ATTN_QBLK), :] = (acc / tot).astype(o_ref.dtype)
            return c

        lax.fori_loop(0, blocks_per_super, merge, 0)
        return carry

    lax.fori_loop(0, seq // ATTN_SUPER, super_block, 0)


def _attn(p3, q_gain, k_gain, tables):
    B, S, _ = p3.shape
    assert S % ATTN_SUPER == 0 and S // DILATIONS[-1] >= ATTN_KWIN
    blk = (None, S, ATTN_HEAD_DIM)
    const = lambda b, h: (0, 0)
    once = dict(pipeline_mode=pl.Buffered(1))
    vmem = 2 * 3 * S * 128 * 4 + 2 * S * 128 * 2 + 4 * S * 128 * 4 + 2 * len(DILATIONS) * ATTN_SUPER * 128 * 4 \
        + 4 * MIB
    return pl.pallas_call(
        functools.partial(_attn_kernel, seq=S),
        grid=(B, ATTN_HEADS),
        in_specs=[
            pl.BlockSpec(blk, lambda b, h: (b, 0, h)),
            pl.BlockSpec(blk, lambda b, h: (b, 0, ATTN_HEADS + h)),
            pl.BlockSpec(blk, lambda b, h: (b, 0, 2 * ATTN_HEADS + h)),
            pl.BlockSpec((1, ATTN_HEAD_DIM), const),
            pl.BlockSpec((1, ATTN_HEAD_DIM), const),
            pl.BlockSpec((S, ATTN_HEAD_DIM), const, **once),
            pl.BlockSpec((S, ATTN_HEAD_DIM), const, **once),
        ],
        out_specs=pl.BlockSpec(blk, lambda b, h: (b, 0, h)),
        out_shape=jax.ShapeDtypeStruct((B, S, ATTN_WIDTH), BF16),
        scratch_shapes=[pltpu.VMEM((S, ATTN_HEAD_DIM), F32),
                        pltpu.VMEM((S, ATTN_HEAD_DIM), F32),
                        pltpu.VMEM((len(DILATIONS), ATTN_SUPER, ATTN_HEAD_DIM), F32),
                        pltpu.VMEM((len(DILATIONS), ATTN_SUPER, ATTN_HEAD_DIM), F32),
                        pltpu.VMEM((3, ATTN_QBLK, ATTN_KWIN), F32)],
        compiler_params=_params(("parallel", "parallel"), vmem, claim_all=True),
        name="attn",
    )(p3, p3, p3, q_gain, k_gain, *tables)


def _decay_tables(lg, backward):
    C = RET_CHUNK
    i = lax.broadcasted_iota(jnp.int32, (C, C), 0)
    j = lax.broadcasted_iota(jnp.int32, (C, C), 1)
    diff = (j - i) if backward else (i - j)
    mask = jnp.where(diff >= 0, jnp.exp(lg * jnp.maximum(diff, 0).astype(F32)), 0.0)
    idx = lax.broadcasted_iota(jnp.int32, (C, 1), 0).astype(F32)
    if backward:
        k_pow, q_pow = idx, C - idx
    else:
        k_pow, q_pow = C - 1 - idx, idx + 1
    chunk = jnp.exp(jnp.full((1, RET_DIM), C, F32) * lg)
    return mask, jnp.exp(lg * k_pow), jnp.exp(lg * q_pow), chunk


def _ret_cross(q, k, vb, state, k_scale, q_scale, chunk_decay):
    cross = jnp.dot((q * q_scale).astype(BF16), state.astype(BF16), preferred_element_type=F32)
    kv = lax.dot_general((k * k_scale).astype(BF16), vb, (((0,), (0,)), ((), ())), preferred_element_type=F32)
    return cross, chunk_decay * state + kv


def _ret_bwd_rows(lg, q_ref, k_ref, v_ref, o_ref, state_ref, row0):
    _, k_scale, q_scale, chunk_decay = _decay_tables(lg, backward=True)
    state = state_ref[...]
    for c in reversed(range(RET_ROWS // RET_CHUNK)):
        rows = slice(row0 + c * RET_CHUNK, row0 + (c + 1) * RET_CHUNK)
        cross, state = _ret_cross(q_ref[rows, :], k_ref[rows, :], v_ref[rows, :].astype(BF16), state,
                                  k_scale, q_scale, chunk_decay)
        o_ref[rows, :] = cross
    state_ref[...] = state


def _ret_fwd_rows(lg_fwd, lg_bwd, q_ref, k_ref, v_ref, gate_ref, bwd_ref, gain_ref, o_ref, state_ref, row0):
    mask_f, k_scale, q_scale, chunk_decay = _decay_tables(lg_fwd, backward=False)
    mask = mask_f + _decay_tables(lg_bwd, backward=True)[0]
    state = state_ref[...]
    for c in range(RET_ROWS // RET_CHUNK):
        rows = slice(row0 + c * RET_CHUNK, row0 + (c + 1) * RET_CHUNK)
        q, k, vb = q_ref[rows, :], k_ref[rows, :], v_ref[rows, :].astype(BF16)
        s = lax.dot_general(q.astype(BF16), k.astype(BF16), (((1,), (1,)), ((), ())), preferred_element_type=F32)
        inner = jnp.dot((s * mask).astype(BF16), vb, preferred_element_type=F32)
        cross, state = _ret_cross(q, k, vb, state, k_scale, q_scale, chunk_decay)
        both = _rms_norm_rows(inner + cross + bwd_ref[rows, :], gain_ref[...])
        o_ref[rows, :] = (jax.nn.silu(gate_ref[rows, :]) * both).astype(o_ref.dtype)
    state_ref[...] = state


def _ret_bwd_kernel(ld_ref, q_ref, k_ref, v_ref, o_ref, state_ref):
    h, t = pl.program_id(1), pl.program_id(2)

    @pl.when(t == 0)
    def _():
        state_ref[...] = jnp.zeros_like(state_ref)

    _ret_bwd_rows(ld_ref[h], q_ref, k_ref, v_ref, o_ref, state_ref, 0)


def _ret_fwd_kernel(ldf_ref, ldb_ref, q_ref, k_ref, v_ref, gate_ref, bwd_ref, gain_ref, o_ref, state_ref):
    h, t = pl.program_id(1), pl.program_id(2)

    @pl.when(t == 0)
    def _():
        state_ref[...] = jnp.zeros_like(state_ref)

    _ret_fwd_rows(ldf_ref[h], ldb_ref[h], q_ref, k_ref, v_ref, gate_ref, bwd_ref, gain_ref, o_ref, state_ref, 0)


def _ret_fused_kernel(ldf_ref, ldb_ref, q_ref, k_ref, v_ref, gate_ref, gain_ref, o_ref, bwd_ref, state_ref, *, seq):
    h = pl.program_id(1)
    blocks = range(0, seq, RET_ROWS)
    state_ref[...] = jnp.zeros_like(state_ref)
    for row0 in reversed(blocks):
        _ret_bwd_rows(ldb_ref[h], q_ref, k_ref, v_ref, bwd_ref, state_ref, row0)
    state_ref[...] = jnp.zeros_like(state_ref)
    for row0 in blocks:
        _ret_fwd_rows(ldf_ref[h], ldb_ref[h], q_ref, k_ref, v_ref, gate_ref, bwd_ref, gain_ref, o_ref, state_ref, row0)


def _ret_specs(n_t, reverse):
    tmap = (lambda t: n_t - 1 - t) if reverse else (lambda t: t)
    blk = (None, RET_ROWS, RET_DIM)
    seg0 = 3 * ATTN_WIDTH // RET_DIM

    def col(seg):
        return pl.BlockSpec(blk, lambda b, h, t: (b, tmap(t), seg0 + seg * RET_HEADS + h))

    return blk, tmap, col


def _ret_bwd(p3, log_decay):
    B, S, _ = p3.shape
    n_t = S // RET_ROWS
    blk, tmap, col = _ret_specs(n_t, reverse=True)
    vmem = 2 * 4 * RET_ROWS * RET_DIM * 4 + RET_DIM * RET_DIM * 4 + 8 * MIB
    return pl.pallas_call(
        _ret_bwd_kernel,
        grid=(B, RET_HEADS, n_t),
        in_specs=[pl.BlockSpec(memory_space=pltpu.SMEM), col(0), col(1), col(2)],
        out_specs=pl.BlockSpec(blk, lambda b, h, t: (b, tmap(t), h)),
        out_shape=jax.ShapeDtypeStruct((B, S, RET_WIDTH), F32),
        scratch_shapes=[pltpu.VMEM((RET_DIM, RET_DIM), F32)],
        compiler_params=_params(("parallel", "parallel", "arbitrary"), vmem, claim_all=True),
        name="ret_bwd",
    )(log_decay, p3, p3, p3)


def _ret_fwd(p3, bwd, log_decay_fwd, log_decay_bwd, out_gain):
    B, S, _ = p3.shape
    n_t = S // RET_ROWS
    blk, tmap, col = _ret_specs(n_t, reverse=False)
    vmem = 2 * 6 * RET_ROWS * RET_DIM * 4 + RET_DIM * RET_DIM * 4 + 8 * MIB
    smem = pl.BlockSpec(memory_space=pltpu.SMEM)
    return pl.pallas_call(
        _ret_fwd_kernel,
        grid=(B, RET_HEADS, n_t),
        in_specs=[smem, smem, col(0), col(1), col(2), col(3),
                  pl.BlockSpec(blk, lambda b, h, t: (b, t, h)),
                  pl.BlockSpec((None, 1, RET_DIM), lambda b, h, t: (h, 0, 0))],
        out_specs=pl.BlockSpec(blk, lambda b, h, t: (b, t, h)),
        out_shape=jax.ShapeDtypeStruct((B, S, RET_WIDTH), BF16),
        scratch_shapes=[pltpu.VMEM((RET_DIM, RET_DIM), F32)],
        compiler_params=_params(("parallel", "parallel", "arbitrary"), vmem, claim_all=True),
        name="ret_fwd",
    )(log_decay_fwd, log_decay_bwd, p3, p3, p3, p3, bwd, out_gain)


def _retention(p3, log_decay_fwd, log_decay_bwd, out_gain):
    B, S, _ = p3.shape
    seq_bytes = S * RET_DIM * 4
    fused_vmem = 2 * 4 * seq_bytes + seq_bytes + 2 * seq_bytes // 2 + RET_DIM * RET_DIM * 4 + 6 * MIB
    if fused_vmem > V7X_SCOPED_VMEM_BYTES:
        bwd = _ret_bwd(p3, log_decay_bwd)
        return _ret_fwd(p3, bwd, log_decay_fwd, log_decay_bwd, out_gain)
    blk = (None, S, RET_DIM)
    seg0 = 3 * ATTN_WIDTH // RET_DIM
    col = lambda seg: pl.BlockSpec(blk, lambda b, h: (b, 0, seg0 + seg * RET_HEADS + h))
    smem = pl.BlockSpec(memory_space=pltpu.SMEM)
    return pl.pallas_call(
        functools.partial(_ret_fused_kernel, seq=S),
        grid=(B, RET_HEADS),
        in_specs=[smem, smem, col(0), col(1), col(2), col(3),
                  pl.BlockSpec((None, 1, RET_DIM), lambda b, h: (h, 0, 0))],
        out_specs=pl.BlockSpec(blk, lambda b, h: (b, 0, h)),
        out_shape=jax.ShapeDtypeStruct((B, S, RET_WIDTH), BF16),
        scratch_shapes=[pltpu.VMEM((S, RET_DIM), F32), pltpu.VMEM((RET_DIM, RET_DIM), F32)],
        compiler_params=_params(("parallel", "parallel"), fused_vmem, claim_all=True),
        name="ret_fused",
    )(log_decay_fwd, log_decay_bwd, p3, p3, p3, p3, out_gain)


def _mix_out_kernel(x_ref, a_ref, r_ref, w_ref, o_ref):
    acc = jnp.dot(a_ref[...], w_ref[:ATTN_WIDTH, :], preferred_element_type=F32)
    acc += jnp.dot(r_ref[...], w_ref[ATTN_WIDTH:, :], preferred_element_type=F32)
    o_ref[...] = x_ref[...] + acc


def _mix_out(x, a, r, w_out):
    T, D = x.shape
    tm = ROW_TILE
    vmem = 2 * (2 * tm * D * 4 + 2 * tm * SEG * 2 + D * D * 2) + tm * D * 4 + 4 * MIB
    return pl.pallas_call(
        _mix_out_kernel,
        grid=(T // tm,),
        in_specs=[
            pl.BlockSpec((tm, D), lambda i: (i, 0)),
            pl.BlockSpec((tm, ATTN_WIDTH), lambda i: (i, 0)),
            pl.BlockSpec((tm, RET_WIDTH), lambda i: (i, 0)),
            pl.BlockSpec((D, D), lambda i: (0, 0)),
        ],
        out_specs=pl.BlockSpec((tm, D), lambda i: (i, 0)),
        out_shape=jax.ShapeDtypeStruct((T, D), F32),
        compiler_params=_params(("parallel",), vmem, claim_all=True),
        name="mix_out",
    )(x, a, r, w_out)


_MIXER_WEIGHTS = ("w_in", "w_out")
_FFN1_WEIGHTS = ("ffn1_wg", "ffn1_wu", "ffn1_wd")
_FFN2_WEIGHTS = ("ffn2_wg", "ffn2_wu", "ffn2_wd")
_WEIGHT_PADS = {"w_in": (D_MODEL, IN_COLS), "w_out": (D_MODEL, D_MODEL),
                "ffn1_wg": (D_MODEL, D_FF_PAD), "ffn1_wu": (D_MODEL, D_FF_PAD), "ffn1_wd": (D_FF_PAD, D_MODEL),
                "ffn2_wg": (D_MODEL, D_FF_PAD), "ffn2_wu": (D_MODEL, D_FF_PAD), "ffn2_wd": (D_FF_PAD, D_MODEL)}


def _trunk(x3, layers, f32_weights, bf16_weights):
    B, S, D = x3.shape
    T = B * S
    x = x3.reshape(T, D)
    attn_tables, ret_tables = _rope_tables(S, ATTN_HEAD_DIM), _rope_tables(S, RET_DIM)

    def ffn_carrying(x, gain, own, layer, carried, carried_layer):
        names = [n for n in carried if carried_layer < len(layers) and (n, carried_layer) not in bf16_weights]
        casts = [_CarriedCast(f32_weights[n], carried_layer, *_WEIGHT_PADS[n]) for n in names]
        y, cast = _ffn(x, gain, *[bf16_weights[(n, layer)] for n in own], casts=casts)
        bf16_weights.update({(n, carried_layer): w for n, w in zip(names, cast)})
        return y

    for l, lw in enumerate(layers):
        x = ffn_carrying(x, lw["ffn1_norm"], _FFN1_WEIGHTS, l, _MIXER_WEIGHTS + _FFN2_WEIGHTS, l)
        proj = _proj(x, lw["mix_norm"], bf16_weights[("w_in", l)], ret_tables, S)
        p3 = proj.reshape(B, S, IN_COLS)
        a = _attn(p3, lw["q_gain"], lw["k_gain"], attn_tables)
        r = _retention(p3, lw["ld_fwd"], lw["ld_bwd"], lw["ret_gain"])
        x = _mix_out(x, a.reshape(T, ATTN_WIDTH), r.reshape(T, RET_WIDTH), bf16_weights[("w_out", l)])
        x = ffn_carrying(x, lw["ffn2_norm"], _FFN2_WEIGHTS, l, _FFN1_WEIGHTS, l + 1)
    return x.reshape(B, S, D)


def kernel(x_prompt, x_sample, ffn1_norm, ffn1_w_gate, ffn1_w_up, ffn1_w_down, mix_norm, w_in, attn_q_norm, attn_k_norm, ret_log_decay_fwd, ret_log_decay_bwd, ret_out_norm, w_out, ffn2_norm, ffn2_w_gate, ffn2_w_up, ffn2_w_down):
    depth = w_in.shape[0]
    layers = []
    for l in range(depth):
        layers.append({
            "ffn1_norm": ffn1_norm[l][None, :].astype(F32),
            "mix_norm": mix_norm[l][None, :].astype(F32),
            "q_gain": attn_q_norm[l][None, :].astype(F32),
            "k_gain": attn_k_norm[l][None, :].astype(F32),
            "ld_fwd": ret_log_decay_fwd[l].astype(F32),
            "ld_bwd": ret_log_decay_bwd[l].astype(F32),
            "ret_gain": ret_out_norm[l][:, None, :].astype(F32),
            "ffn2_norm": ffn2_norm[l][None, :].astype(F32),
        })
    f32_weights = {"w_in": w_in, "w_out": w_out,
                   "ffn1_wg": ffn1_w_gate, "ffn1_wu": ffn1_w_up, "ffn1_wd": ffn1_w_down,
                   "ffn2_wg": ffn2_w_gate, "ffn2_wu": ffn2_w_up, "ffn2_wd": ffn2_w_down}
    bf16_weights = {(n, 0): _cast_layer(f32_weights[n], 0, *_WEIGHT_PADS[n]) for n in _FFN1_WEIGHTS}
    y_sample = _trunk(x_sample, layers, f32_weights, bf16_weights)
    y_prompt = _trunk(x_prompt, layers, f32_weights, bf16_weights)
    return (y_prompt, y_sample)
```

```python
import functools
from typing import NamedTuple

import jax
import jax.numpy as jnp
from jax import lax
from jax.experimental import pallas as pl
from jax.experimental.pallas import tpu as pltpu

F32 = jnp.float32
BF16 = jnp.bfloat16

D_MODEL = 2048
ATTN_HEAD_DIM = 128
ATTN_HEADS = 8
ATTN_WIDTH = ATTN_HEADS * ATTN_HEAD_DIM
DILATIONS = (1, 4, 16)
ATTN_HALF = 64
RET_HEADS = 4
RET_DIM = 256
RET_WIDTH = RET_HEADS * RET_DIM
RET_CHUNK = 128
D_FF = 5504
IN_COLS = 3 * ATTN_WIDTH + 4 * RET_WIDTH
ROPE_THETA = 10000.0
NORM_EPS = 1e-6
NEG_INF = -1e30
LOG2_E = 1.4426950408889634

V7X_LANES = 128
V7X_BF16_SUBLANES = 16
V7X_SCOPED_VMEM_BYTES = 56 * 1024 * 1024
FF_TILE = 512
D_FF_PAD = -(-D_FF // FF_TILE) * FF_TILE
MIX_ROWS = 512
FFN_ROWS = 1024
PROJ_ROWS = 1024
CAST_TILE_ELEMS = 1024 * 1024
SEG = 1024
PROJ_CHUNK = RET_DIM
ATTN_QBLK = 128
ATTN_KWIN = 256
ATTN_SUPER = ATTN_QBLK * DILATIONS[-1]
BY4 = DILATIONS[1]
ATTN_UNROLL = 16
RET_ROWS = 1024
MIB = 1024 * 1024


def _params(semantics, vmem_bytes, claim_all=False):
    if claim_all:
        assert vmem_bytes <= V7X_SCOPED_VMEM_BYTES
        vmem_bytes = V7X_SCOPED_VMEM_BYTES
    return pltpu.CompilerParams(dimension_semantics=semantics, vmem_limit_bytes=int(vmem_bytes))


def _rms_norm_rows(x, gain):
    return x * lax.rsqrt(jnp.mean(x * x, axis=-1, keepdims=True) + NORM_EPS) * gain


def _cast_tile(w_ref, o_ref, row_tile, col_tile, rows, cols):
    tr, tc = o_ref.shape
    w = w_ref[...]
    if rows % tr or cols % tc:
        r = lax.broadcasted_iota(jnp.int32, (tr, tc), 0) + row_tile * tr
        c = lax.broadcasted_iota(jnp.int32, (tr, tc), 1) + col_tile * tc
        w = jnp.where((r < rows) & (c < cols), w, 0.0)
    o_ref[...] = w.astype(o_ref.dtype)


def _cast_kernel(w_ref, o_ref, *, rows, cols):
    _cast_tile(w_ref, o_ref, pl.program_id(0), pl.program_id(1), rows, cols)


class _CarriedCast(NamedTuple):
    w: jax.Array
    layer: int
    rows_pad: int
    cols_pad: int

    def tiling(self, n_row_blocks, n_col_steps):
        tr = self.rows_pad // n_row_blocks
        n_col_tiles = max(n for n in range(1, n_col_steps + 1)
                          if self.cols_pad % n == 0 and (self.cols_pad // n) % V7X_LANES == 0)
        assert self.rows_pad % n_row_blocks == 0 and tr % V7X_BF16_SUBLANES == 0
        return tr, self.cols_pad // n_col_tiles, n_col_tiles

    def specs(self, n_row_blocks, n_col_steps):
        tr, tc, n_col_tiles = self.tiling(n_row_blocks, n_col_steps)
        layer = self.layer
        col = lambda f: jnp.minimum(f, n_col_tiles - 1)
        return (pl.BlockSpec((None, tr, tc), lambda i, f: (layer, i, col(f))),
                pl.BlockSpec((tr, tc), lambda i, f: (i, col(f))),
                jax.ShapeDtypeStruct((self.rows_pad, self.cols_pad), BF16))


def _run_carried_casts(extents, in_refs, out_refs, i, f):
    for (rows, cols, n_col_tiles), w_ref, o_ref in zip(extents, in_refs, out_refs):
        _cast_tile(w_ref, o_ref, i, jnp.minimum(f, n_col_tiles - 1), rows, cols)


def _cast_layer(w, layer, rows_pad, cols_pad):
    _, rows, cols = w.shape
    tc = next(c for c in (2048, 1024, 512) if cols_pad % c == 0)
    tr = CAST_TILE_ELEMS // tc
    assert rows_pad % tr == 0
    return pl.pallas_call(
        functools.partial(_cast_kernel, rows=rows, cols=cols),
        grid=(rows_pad // tr, cols_pad // tc),
        in_specs=[pl.BlockSpec((None, tr, tc), lambda i, j: (layer, i, j))],
        out_specs=pl.BlockSpec((tr, tc), lambda i, j: (i, j)),
        out_shape=jax.ShapeDtypeStruct((rows_pad, cols_pad), BF16),
        compiler_params=_params(("parallel", "parallel"), 2 * tr * tc * (4 + 2) + 4 * MIB, claim_all=True),
        name="cast",
    )(w)


def _ffn_kernel(x_ref, g_ref, wg_ref, wu_ref, wd_ref, *refs, cast_extents):
    n_casts = len(cast_extents)
    cast_in, o_ref, cast_out, xn_ref = refs[:n_casts], refs[n_casts], refs[n_casts + 1:-1], refs[-1]
    i, f = pl.program_id(0), pl.program_id(1)
    last = pl.num_programs(1) - 1
    tm = x_ref.shape[0]

    def step(rows, first, final):
        if first:
            xn_ref[rows, :] = _rms_norm_rows(x_ref[rows, :], g_ref[...]).astype(BF16)
        xn = xn_ref[rows, :]
        gate = jnp.dot(xn, wg_ref[...], preferred_element_type=F32)
        up = jnp.dot(xn, wu_ref[...], preferred_element_type=F32)
        h = (jax.nn.silu(gate) * up).astype(BF16)
        acc = jnp.dot(h, wd_ref[...], preferred_element_type=F32)
        if not first:
            acc = o_ref[rows, :] + acc
        o_ref[rows, :] = x_ref[rows, :] + 0.5 * acc if final else acc

    halves = (slice(0, tm // 2), slice(tm // 2, tm))

    @pl.when(f == 0)
    def _():
        _run_carried_casts(cast_extents, cast_in, cast_out, i, f)
        for rows in halves:
            step(rows, True, False)

    @pl.when((f > 0) & (f < last))
    def _():
        _run_carried_casts(cast_extents, cast_in, cast_out, i, f)
        step(slice(0, tm), False, False)

    @pl.when(f == last)
    def _():
        _run_carried_casts(cast_extents, cast_in, cast_out, i, f)
        for rows in halves:
            step(rows, False, True)


def _ffn(x, gain, wg, wu, wd, casts=()):
    T, D = x.shape
    tm, tf = FFN_ROWS, FF_TILE
    grid = (T // tm, D_FF_PAD // tf)
    cast_specs = [c.specs(*grid) for c in casts]
    cast_extents = tuple(c.w.shape[1:] + (c.tiling(*grid)[2],) for c in casts)
    cast_vmem = sum(2 * tr * tc * (4 + 2) for tr, tc, _ in (c.tiling(*grid) for c in casts))
    vmem = 2 * (2 * tm * D * 4) + tm * D * 2 + 2 * 3 * D * tf * 2 + 3 * tm * tf * 4 + cast_vmem + 2 * MIB
    y, *cast_out = pl.pallas_call(
        functools.partial(_ffn_kernel, cast_extents=cast_extents),
        grid=grid,
        in_specs=[
            pl.BlockSpec((tm, D), lambda i, f: (i, 0)),
            pl.BlockSpec((1, D), lambda i, f: (0, 0)),
            pl.BlockSpec((D, tf), lambda i, f: (0, f)),
            pl.BlockSpec((D, tf), lambda i, f: (0, f)),
            pl.BlockSpec((tf, D), lambda i, f: (f, 0)),
        ] + [s[0] for s in cast_specs],
        out_specs=[pl.BlockSpec((tm, D), lambda i, f: (i, 0))] + [s[1] for s in cast_specs],
        out_shape=[jax.ShapeDtypeStruct((T, D), F32)] + [s[2] for s in cast_specs],
        scratch_shapes=[pltpu.VMEM((tm, D), BF16)],
        compiler_params=_params(("parallel", "arbitrary"), vmem),
        name="ffn",
    )(x, gain, wg, wu, wd, *[c.w for c in casts])
    return y, cast_out


def _proj_kernel(x_ref, g_ref, w_ref, cr_ref, sr_ref, o_ref, xn_ref):
    n = pl.program_id(1)

    tm = x_ref.shape[0]

    def by_chunks(epilogue, rows=slice(None)):
        xn = xn_ref[rows, :]
        for j in range(SEG // PROJ_CHUNK):
            c0 = j * PROJ_CHUNK
            y = jnp.dot(xn, w_ref[:, c0:c0 + PROJ_CHUNK], preferred_element_type=F32)
            epilogue(y, c0, rows)

    def ret_qk(scale):
        def epilogue(y, c0, rows):
            cos, sin = cr_ref[rows, :], sr_ref[rows, :]
            half = RET_DIM // 2
            lo, hi = y[:, :half], y[:, half:]
            o_ref[rows, c0:c0 + half] = (lo * cos[:, :half] + hi * sin[:, :half]) * scale
            o_ref[rows, c0 + half:c0 + RET_DIM] = (hi * cos[:, half:] + lo * sin[:, half:]) * scale
        by_chunks(epilogue)

    def plain(rows=slice(None)):
        o_ref[rows, :] = jnp.dot(xn_ref[rows, :], w_ref[...], preferred_element_type=F32)

    @pl.when(n == 0)
    def _():
        for rows in (slice(0, tm // 2), slice(tm // 2, tm)):
            xn_ref[rows, :] = _rms_norm_rows(x_ref[rows, :], g_ref[...]).astype(BF16)
            plain(rows)

    @pl.when(n == 3)
    def _():
        ret_qk(1.0)

    @pl.when(n == 4)
    def _():
        ret_qk(RET_DIM ** -0.5)

    @pl.when(((n > 0) & (n < 3)) | (n >= 5))
    def _():
        plain()


def _rope_tables(seq, head_dim):
    half = head_dim // 2
    inv_freq = ROPE_THETA ** (-jnp.arange(half, dtype=F32) / half)
    ang = jnp.arange(seq, dtype=F32)[:, None] * inv_freq[None, :]
    cos, sin = jnp.cos(ang), jnp.sin(ang)
    return jnp.concatenate([cos, cos], axis=-1), jnp.concatenate([-sin, sin], axis=-1)


def _proj(x, gain, w_in, ret_tables, seq):
    T, D = x.shape
    tm = PROJ_ROWS
    nseq = seq // tm
    cr, sr = ret_tables
    pos = lambda i, n: (i % nseq, 0)
    const = lambda i, n: (0, 0)
    vmem = 2 * tm * D * 4 + tm * D * 2 + 2 * D * SEG * 2 + 2 * tm * SEG * 4 + 2 * 2 * tm * RET_DIM * 4 \
        + 6 * tm * PROJ_CHUNK * 4 + 4 * MIB
    return pl.pallas_call(
        _proj_kernel,
        grid=(T // tm, IN_COLS // SEG),
        in_specs=[
            pl.BlockSpec((tm, D), lambda i, n: (i, 0)),
            pl.BlockSpec((1, D), const),
            pl.BlockSpec((D, SEG), lambda i, n: (0, n)),
            pl.BlockSpec((tm, RET_DIM), pos),
            pl.BlockSpec((tm, RET_DIM), pos),
        ],
        out_specs=pl.BlockSpec((tm, SEG), lambda i, n: (i, n)),
        out_shape=jax.ShapeDtypeStruct((T, IN_COLS), F32),
        scratch_shapes=[pltpu.VMEM((tm, D), BF16)],
        compiler_params=_params(("parallel", "arbitrary"), vmem),
        name="proj",
    )(x, gain, w_in, cr, sr)


def _attn_kernel(qraw_ref, kraw_ref, v_ref, qg_ref, kg_ref, cos_ref, sin_ref, o_ref,
                 q_ref, k_ref, po_ref, pl_ref, bias_ref, *by4_refs, seq):
    scale = ATTN_HEAD_DIM ** -0.5
    exp2_scale = scale * LOG2_E
    blocks_per_super = ATTN_SUPER // ATTN_QBLK

    rel = (lax.broadcasted_iota(jnp.int32, (ATTN_QBLK, ATTN_KWIN), 0)
           - lax.broadcasted_iota(jnp.int32, (ATTN_QBLK, ATTN_KWIN), 1))
    for case in range(3):
        bias_ref[case] = jnp.where(jnp.abs(rel + case * ATTN_HALF) <= ATTN_HALF, 0.0, NEG_INF)

    ones = jnp.ones((ATTN_HEAD_DIM, ATTN_HEAD_DIM), BF16)

    def row_sums(t):
        hi = t.astype(BF16)
        lo = (t - hi.astype(F32)).astype(BF16)
        return jnp.dot(hi, ones, preferred_element_type=F32) + jnp.dot(lo, ones, preferred_element_type=F32)

    def qk_norm_rope(i, c):
        rows = pl.ds(pl.multiple_of(i * ATTN_KWIN, ATTN_KWIN), ATTN_KWIN)
        cos, sin = cos_ref[rows, :], sin_ref[rows, :]
        for raw_ref, gain_ref, dst_ref in ((qraw_ref, qg_ref, q_ref), (kraw_ref, kg_ref, k_ref)):
            y = raw_ref[rows, :]
            mean_sq = row_sums(y * y) * (1.0 / ATTN_HEAD_DIM)
            yn = y * lax.rsqrt(mean_sq + NORM_EPS) * gain_ref[...]
            dst_ref[rows, :] = yn * cos + pltpu.roll(yn, ATTN_HEAD_DIM // 2, 1) * sin
        return c

    by4_ref = by4_refs[0] if by4_refs else None
    n_chunks = seq // ATTN_KWIN

    def regroup(chunk):
        c0 = pl.multiple_of(chunk * ATTN_KWIN, ATTN_KWIN)
        n = ATTN_KWIN // BY4
        dst = pl.ds(pl.multiple_of(chunk * n, n), n)
        for t, src_ref in enumerate((q_ref, k_ref, v_ref)):
            for r in range(BY4):
                by4_ref[t, r, dst, :] = src_ref[pl.ds(c0 + r, n, stride=BY4), :]

    if by4_ref is None:
        lax.fori_loop(0, n_chunks, qk_norm_rope, 0, unroll=4)
    else:
        def regroup_then_norm(t, c):
            regroup(t - 1)
            return qk_norm_rope(t, c)

        qk_norm_rope(jnp.int32(0), 0)
        lax.fori_loop(1, n_chunks, regroup_then_norm, 0, unroll=4)
        regroup(jnp.int32(n_chunks - 1))

    def super_block(sb, carry):
        base = pl.multiple_of(sb * ATTN_SUPER, ATTN_SUPER)

        def blocks(u, c):
            for p, d in enumerate(DILATIONS):
                sub_len = seq // d
                per_res = blocks_per_super // d
                r = u % d
                jl = u // d
                jb = sb * per_res + jl
                q0 = jb * ATTN_QBLK
                k0 = jnp.clip(q0 - ATTN_HALF, 0, sub_len - ATTN_KWIN)
                if d == 1:
                    qs, ks = pl.ds(q0, ATTN_QBLK), pl.ds(k0, ATTN_KWIN)
                    os = pl.ds(jl * ATTN_QBLK, ATTN_QBLK)
                else:
                    qs = pl.ds(r + d * q0, ATTN_QBLK, stride=d)
                    ks = pl.ds(r + d * k0, ATTN_KWIN, stride=d)
                    os = pl.ds(r + d * jl * ATTN_QBLK, ATTN_QBLK, stride=d)
                if by4_ref is not None and d % BY4 == 0:
                    sub, lane_r, off = d // BY4, r % BY4, r // BY4
                    if sub == 1:
                        qs4, ks4 = pl.ds(q0, ATTN_QBLK), pl.ds(k0, ATTN_KWIN)
                    else:
                        qs4 = pl.ds(off + sub * q0, ATTN_QBLK, stride=sub)
                        ks4 = pl.ds(off + sub * k0, ATTN_KWIN, stride=sub)
                    q = by4_ref[0, lane_r, qs4, :].astype(BF16)
                    k = by4_ref[1, lane_r, ks4, :].astype(BF16)
                    v = by4_ref[2, lane_r, ks4, :].astype(BF16)
                else:
                    q = q_ref[qs, :].astype(BF16)
                    k = k_ref[ks, :].astype(BF16)
                    v = v_ref[ks, :].astype(BF16)
                s = lax.dot_general(q, k, (((1,), (1,)), ((), ())), preferred_element_type=F32)
                s = s + bias_ref[(q0 - k0) // ATTN_HALF]
                m = jnp.max(s, axis=-1, keepdims=True)
                e = jnp.exp2((s - m) * exp2_scale)
                den = jnp.sum(e, axis=-1, keepdims=True)
                out = jnp.dot(e.astype(BF16), v, preferred_element_type=F32) / den
                po_ref[p, os, :] = out
                pl_ref[p, os, :] = jnp.broadcast_to(m * exp2_scale + jnp.log2(den), (ATTN_QBLK, ATTN_HEAD_DIM))
            return c

        lax.fori_loop(0, blocks_per_super, blocks, 0, unroll=ATTN_UNROLL)

        def merge(i, c):
            rows = pl.ds(pl.multiple_of(i * ATTN_QBLK, ATTN_QBLK), ATTN_QBLK)
            lse = [pl_ref[p, rows, :] for p in range(len(DILATIONS))]
            top = functools.reduce(jnp.maximum, lse)
            w = [jnp.exp2(l - top) for l in lse]
            tot = functools.reduce(jnp.add, w)
            acc = functools.reduce(jnp.add, [w[p] * po_ref[p, rows, :] for p in range(len(DILATIONS))])
            o_ref[pl.ds(base + i * ATTN_QBLK, ATTN_QBLK), :] = (acc / tot).astype(o_ref.dtype)
            return c

        lax.fori_loop(0, blocks_per_super, merge, 0)
        return carry

    lax.fori_loop(0, seq // ATTN_SUPER, super_block, 0)


def _attn(p3, q_gain, k_gain, tables):
    B, S, _ = p3.shape
    assert S % ATTN_SUPER == 0 and S // DILATIONS[-1] >= ATTN_KWIN
    blk = (None, S, ATTN_HEAD_DIM)
    const = lambda b, h: (0, 0)
    once = dict(pipeline_mode=pl.Buffered(1))
    vmem = 2 * 3 * S * 128 * 4 + 2 * S * 128 * 2 + 4 * S * 128 * 4 + 2 * len(DILATIONS) * ATTN_SUPER * 128 * 4 \
        + 4 * MIB
    by4_bytes = 3 * S * ATTN_HEAD_DIM * 4
    by4 = [pltpu.VMEM((3, BY4, S // BY4, ATTN_HEAD_DIM), F32)] if vmem + by4_bytes <= V7X_SCOPED_VMEM_BYTES else []
    vmem += by4_bytes * len(by4)
    return pl.pallas_call(
        functools.partial(_attn_kernel, seq=S),
        grid=(B, ATTN_HEADS),
        in_specs=[
            pl.BlockSpec(blk, lambda b, h: (b, 0, h)),
            pl.BlockSpec(blk, lambda b, h: (b, 0, ATTN_HEADS + h)),
            pl.BlockSpec(blk, lambda b, h: (b, 0, 2 * ATTN_HEADS + h)),
            pl.BlockSpec((1, ATTN_HEAD_DIM), const),
            pl.BlockSpec((1, ATTN_HEAD_DIM), const),
            pl.BlockSpec((S, ATTN_HEAD_DIM), const, **once),
            pl.BlockSpec((S, ATTN_HEAD_DIM), const, **once),
        ],
        out_specs=pl.BlockSpec(blk, lambda b, h: (b, 0, h)),
        out_shape=jax.ShapeDtypeStruct((B, S, ATTN_WIDTH), BF16),
        scratch_shapes=[pltpu.VMEM((S, ATTN_HEAD_DIM), F32),
                        pltpu.VMEM((S, ATTN_HEAD_DIM), F32),
                        pltpu.VMEM((len(DILATIONS), ATTN_SUPER, ATTN_HEAD_DIM), F32),
                        pltpu.VMEM((len(DILATIONS), ATTN_SUPER, ATTN_HEAD_DIM), F32),
                        pltpu.VMEM((3, ATTN_QBLK, ATTN_KWIN), F32)] + by4,
        compiler_params=_params(("parallel", "parallel"), vmem, claim_all=True),
        name="attn",
    )(p3, p3, p3, q_gain, k_gain, *tables)


def _decay_tables(lg, backward):
    C = RET_CHUNK
    i = lax.broadcasted_iota(jnp.int32, (C, C), 0)
    j = lax.broadcasted_iota(jnp.int32, (C, C), 1)
    diff = (j - i) if backward else (i - j)
    mask = jnp.where(diff >= 0, jnp.exp(lg * jnp.maximum(diff, 0).astype(F32)), 0.0)
    idx = lax.broadcasted_iota(jnp.int32, (C, 1), 0).astype(F32)
    if backward:
        k_pow, q_pow = idx, C - idx
    else:
        k_pow, q_pow = C - 1 - idx, idx + 1
    chunk = jnp.exp(jnp.full((1, RET_DIM), C, F32) * lg)
    return mask, jnp.exp(lg * k_pow), jnp.exp(lg * q_pow), chunk


def _ret_cross(q, k, vb, state, k_scale, q_scale, chunk_decay):
    cross = jnp.dot((q * q_scale).astype(BF16), state.astype(BF16), preferred_element_type=F32)
    kv = lax.dot_general((k * k_scale).astype(BF16), vb, (((0,), (0,)), ((), ())), preferred_element_type=F32)
    return cross, chunk_decay * state + kv


def _ret_bwd_rows(lg, q_ref, k_ref, v_ref, o_ref, state_ref, row0):
    _, k_scale, q_scale, chunk_decay = _decay_tables(lg, backward=True)
    state = state_ref[...]
    for c in reversed(range(RET_ROWS // RET_CHUNK)):
        rows = slice(row0 + c * RET_CHUNK, row0 + (c + 1) * RET_CHUNK)
        cross, state = _ret_cross(q_ref[rows, :], k_ref[rows, :], v_ref[rows, :].astype(BF16), state,
                                  k_scale, q_scale, chunk_decay)
        o_ref[rows, :] = cross
    state_ref[...] = state


def _ret_fwd_rows(lg_fwd, lg_bwd, q_ref, k_ref, v_ref, gate_ref, bwd_ref, gain_ref, o_ref, state_ref, row0):
    mask_f, k_scale, q_scale, chunk_decay = _decay_tables(lg_fwd, backward=False)
    mask = mask_f + _decay_tables(lg_bwd, backward=True)[0]
    state = state_ref[...]
    for c in range(RET_ROWS // RET_CHUNK):
        rows = slice(row0 + c * RET_CHUNK, row0 + (c + 1) * RET_CHUNK)
        q, k, vb = q_ref[rows, :], k_ref[rows, :], v_ref[rows, :].astype(BF16)
        s = lax.dot_general(q.astype(BF16), k.astype(BF16), (((1,), (1,)), ((), ())), preferred_element_type=F32)
        inner = jnp.dot((s * mask).astype(BF16), vb, preferred_element_type=F32)
        cross, state = _ret_cross(q, k, vb, state, k_scale, q_scale, chunk_decay)
        both = _rms_norm_rows(inner + cross + bwd_ref[rows, :], gain_ref[...])
        o_ref[rows, :] = (jax.nn.silu(gate_ref[rows, :]) * both).astype(o_ref.dtype)
    state_ref[...] = state


def _ret_bwd_kernel(ld_ref, q_ref, k_ref, v_ref, o_ref, state_ref):
    h, t = pl.program_id(1), pl.program_id(2)

    @pl.when(t == 0)
    def _():
        state_ref[...] = jnp.zeros_like(state_ref)

    _ret_bwd_rows(ld_ref[h], q_ref, k_ref, v_ref, o_ref, state_ref, 0)


def _ret_fwd_kernel(ldf_ref, ldb_ref, q_ref, k_ref, v_ref, gate_ref, bwd_ref, gain_ref, o_ref, state_ref):
    h, t = pl.program_id(1), pl.program_id(2)

    @pl.when(t == 0)
    def _():
        state_ref[...] = jnp.zeros_like(state_ref)

    _ret_fwd_rows(ldf_ref[h], ldb_ref[h], q_ref, k_ref, v_ref, gate_ref, bwd_ref, gain_ref, o_ref, state_ref, 0)


def _ret_fused_kernel(ldf_ref, ldb_ref, q_ref, k_ref, v_ref, gate_ref, gain_ref, o_ref, bwd_ref, state_ref, *, seq):
    h = pl.program_id(1)
    blocks = range(0, seq, RET_ROWS)
    state_ref[...] = jnp.zeros_like(state_ref)
    for row0 in reversed(blocks):
        _ret_bwd_rows(ldb_ref[h], q_ref, k_ref, v_ref, bwd_ref, state_ref, row0)
    state_ref[...] = jnp.zeros_like(state_ref)
    for row0 in blocks:
        _ret_fwd_rows(ldf_ref[h], ldb_ref[h], q_ref, k_ref, v_ref, gate_ref, bwd_ref, gain_ref, o_ref, state_ref, row0)


def _ret_specs(n_t, reverse):
    tmap = (lambda t: n_t - 1 - t) if reverse else (lambda t: t)
    blk = (None, RET_ROWS, RET_DIM)
    seg0 = 3 * ATTN_WIDTH // RET_DIM

    def col(seg):
        return pl.BlockSpec(blk, lambda b, h, t: (b, tmap(t), seg0 + seg * RET_HEADS + h))

    return blk, tmap, col


def _ret_bwd(p3, log_decay):
    B, S, _ = p3.shape
    n_t = S // RET_ROWS
    blk, tmap, col = _ret_specs(n_t, reverse=True)
    vmem = 2 * 4 * RET_ROWS * RET_DIM * 4 + RET_DIM * RET_DIM * 4 + 8 * MIB
    return pl.pallas_call(
        _ret_bwd_kernel,
        grid=(B, RET_HEADS, n_t),
        in_specs=[pl.BlockSpec(memory_space=pltpu.SMEM), col(0), col(1), col(2)],
        out_specs=pl.BlockSpec(blk, lambda b, h, t: (b, tmap(t), h)),
        out_shape=jax.ShapeDtypeStruct((B, S, RET_WIDTH), F32),
        scratch_shapes=[pltpu.VMEM((RET_DIM, RET_DIM), F32)],
        compiler_params=_params(("parallel", "parallel", "arbitrary"), vmem, claim_all=True),
        name="ret_bwd",
    )(log_decay, p3, p3, p3)


def _ret_fwd(p3, bwd, log_decay_fwd, log_decay_bwd, out_gain):
    B, S, _ = p3.shape
    n_t = S // RET_ROWS
    blk, tmap, col = _ret_specs(n_t, reverse=False)
    vmem = 2 * 6 * RET_ROWS * RET_DIM * 4 + RET_DIM * RET_DIM * 4 + 8 * MIB
    smem = pl.BlockSpec(memory_space=pltpu.SMEM)
    return pl.pallas_call(
        _ret_fwd_kernel,
        grid=(B, RET_HEADS, n_t),
        in_specs=[smem, smem, col(0), col(1), col(2), col(3),
                  pl.BlockSpec(blk, lambda b, h, t: (b, t, h)),
                  pl.BlockSpec((None, 1, RET_DIM), lambda b, h, t: (h, 0, 0))],
        out_specs=pl.BlockSpec(blk, lambda b, h, t: (b, t, h)),
        out_shape=jax.ShapeDtypeStruct((B, S, RET_WIDTH), BF16),
        scratch_shapes=[pltpu.VMEM((RET_DIM, RET_DIM), F32)],
        compiler_params=_params(("parallel", "parallel", "arbitrary"), vmem, claim_all=True),
        name="ret_fwd",
    )(log_decay_fwd, log_decay_bwd, p3, p3, p3, p3, bwd, out_gain)


def _retention(p3, log_decay_fwd, log_decay_bwd, out_gain):
    B, S, _ = p3.shape
    seq_bytes = S * RET_DIM * 4
    fused_vmem = 2 * 4 * seq_bytes + seq_bytes + 2 * seq_bytes // 2 + RET_DIM * RET_DIM * 4 + 6 * MIB
    if fused_vmem > V7X_SCOPED_VMEM_BYTES:
        bwd = _ret_bwd(p3, log_decay_bwd)
        return _ret_fwd(p3, bwd, log_decay_fwd, log_decay_bwd, out_gain)
    blk = (None, S, RET_DIM)
    seg0 = 3 * ATTN_WIDTH // RET_DIM
    col = lambda seg: pl.BlockSpec(blk, lambda b, h: (b, 0, seg0 + seg * RET_HEADS + h))
    smem = pl.BlockSpec(memory_space=pltpu.SMEM)
    return pl.pallas_call(
        functools.partial(_ret_fused_kernel, seq=S),
        grid=(B, RET_HEADS),
        in_specs=[smem, smem, col(0), col(1), col(2), col(3),
                  pl.BlockSpec((None, 1, RET_DIM), lambda b, h: (h, 0, 0))],
        out_specs=pl.BlockSpec(blk, lambda b, h: (b, 0, h)),
        out_shape=jax.ShapeDtypeStruct((B, S, RET_WIDTH), BF16),
        scratch_shapes=[pltpu.VMEM((S, RET_DIM), F32), pltpu.VMEM((RET_DIM, RET_DIM), F32)],
        compiler_params=_params(("parallel", "parallel"), fused_vmem, claim_all=True),
        name="ret_fused",
    )(log_decay_fwd, log_decay_bwd, p3, p3, p3, p3, out_gain)


def _mix_out_kernel(x_ref, a_ref, r_ref, w_ref, o_ref):
    acc = jnp.dot(a_ref[...], w_ref[:ATTN_WIDTH, :], preferred_element_type=F32)
    acc += jnp.dot(r_ref[...], w_ref[ATTN_WIDTH:, :], preferred_element_type=F32)
    o_ref[...] = x_ref[...] + acc


def _mix_out(x, a, r, w_out):
    T, D = x.shape
    tm = MIX_ROWS
    vmem = 2 * (2 * tm * D * 4 + 2 * tm * SEG * 2) + D * D * 2 + tm * D * 4 + 4 * MIB
    return pl.pallas_call(
        _mix_out_kernel,
        grid=(T // tm,),
        in_specs=[
            pl.BlockSpec((tm, D), lambda i: (i, 0)),
            pl.BlockSpec((tm, ATTN_WIDTH), lambda i: (i, 0)),
            pl.BlockSpec((tm, RET_WIDTH), lambda i: (i, 0)),
            pl.BlockSpec((D, D), lambda i: (0, 0), pipeline_mode=pl.Buffered(1)),
        ],
        out_specs=pl.BlockSpec((tm, D), lambda i: (i, 0)),
        out_shape=jax.ShapeDtypeStruct((T, D), F32),
        compiler_params=_params(("parallel",), vmem, claim_all=True),
        name="mix_out",
    )(x, a, r, w_out)


_MIXER_WEIGHTS = ("w_in", "w_out")
_FFN1_WEIGHTS = ("ffn1_wg", "ffn1_wu", "ffn1_wd")
_FFN2_WEIGHTS = ("ffn2_wg", "ffn2_wu", "ffn2_wd")
_WEIGHT_PADS = {"w_in": (D_MODEL, IN_COLS), "w_out": (D_MODEL, D_MODEL),
                "ffn1_wg": (D_MODEL, D_FF_PAD), "ffn1_wu": (D_MODEL, D_FF_PAD), "ffn1_wd": (D_FF_PAD, D_MODEL),
                "ffn2_wg": (D_MODEL, D_FF_PAD), "ffn2_wu": (D_MODEL, D_FF_PAD), "ffn2_wd": (D_FF_PAD, D_MODEL)}


def _trunk(x3, layers, f32_weights, bf16_weights):
    B, S, D = x3.shape
    T = B * S
    x = x3.reshape(T, D)
    attn_tables, ret_tables = _rope_tables(S, ATTN_HEAD_DIM), _rope_tables(S, RET_DIM)

    def ffn_carrying(x, gain, own, layer, carried, carried_layer):
        names = [n for n in carried if carried_layer < len(layers) and (n, carried_layer) not in bf16_weights]
        casts = [_CarriedCast(f32_weights[n], carried_layer, *_WEIGHT_PADS[n]) for n in names]
        y, cast = _ffn(x, gain, *[bf16_weights[(n, layer)] for n in own], casts=casts)
        bf16_weights.update({(n, carried_layer): w for n, w in zip(names, cast)})
        return y

    for l, lw in enumerate(layers):
        x = ffn_carrying(x, lw["ffn1_norm"], _FFN1_WEIGHTS, l, _MIXER_WEIGHTS + _FFN2_WEIGHTS, l)
        proj = _proj(x, lw["mix_norm"], bf16_weights[("w_in", l)], ret_tables, S)
        p3 = proj.reshape(B, S, IN_COLS)
        a = _attn(p3, lw["q_gain"], lw["k_gain"], attn_tables)
        r = _retention(p3, lw["ld_fwd"], lw["ld_bwd"], lw["ret_gain"])
        x = _mix_out(x, a.reshape(T, ATTN_WIDTH), r.reshape(T, RET_WIDTH), bf16_weights[("w_out", l)])
        x = ffn_carrying(x, lw["ffn2_norm"], _FFN2_WEIGHTS, l, _FFN1_WEIGHTS, l + 1)
    return x.reshape(B, S, D)


def kernel(x_prompt, x_sample, ffn1_norm, ffn1_w_gate, ffn1_w_up, ffn1_w_down, mix_norm, w_in, attn_q_norm, attn_k_norm, ret_log_decay_fwd, ret_log_decay_bwd, ret_out_norm, w_out, ffn2_norm, ffn2_w_gate, ffn2_w_up, ffn2_w_down):
    depth = w_in.shape[0]
    layers = []
    for l in range(depth):
        layers.append({
            "ffn1_norm": ffn1_norm[l][None, :].astype(F32),
            "mix_norm": mix_norm[l][None, :].astype(F32),
            "q_gain": attn_q_norm[l][None, :].astype(F32),
            "k_gain": attn_k_norm[l][None, :].astype(F32),
            "ld_fwd": ret_log_decay_fwd[l].astype(F32),
            "ld_bwd": ret_log_decay_bwd[l].astype(F32),
            "ret_gain": ret_out_norm[l][:, None, :].astype(F32),
            "ffn2_norm": ffn2_norm[l][None, :].astype(F32),
        })
    f32_weights = {"w_in": w_in, "w_out": w_out,
                   "ffn1_wg": ffn1_w_gate, "ffn1_wu": ffn1_w_up, "ffn1_wd": ffn1_w_down,
                   "ffn2_wg": ffn2_w_gate, "ffn2_wu": ffn2_w_up, "ffn2_wd": ffn2_w_down}
    bf16_weights = {(n, 0): _cast_layer(f32_weights[n], 0, *_WEIGHT_PADS[n]) for n in _FFN1_WEIGHTS}
    y_sample = _trunk(x_sample, layers, f32_weights, bf16_weights)
    y_prompt = _trunk(x_prompt, layers, f32_weights, bf16_weights)
    return (y_prompt, y_sample)
```

```python
import functools
from typing import NamedTuple

import jax
import jax.numpy as jnp
from jax import lax
from jax.experimental import pallas as pl
from jax.experimental.pallas import tpu as pltpu

F32 = jnp.float32
BF16 = jnp.bfloat16

D_MODEL = 2048
ATTN_HEAD_DIM = 128
ATTN_HEADS = 8
ATTN_WIDTH = ATTN_HEADS * ATTN_HEAD_DIM
DILATIONS = (1, 4, 16)
ATTN_HALF = 64
RET_HEADS = 4
RET_DIM = 256
RET_WIDTH = RET_HEADS * RET_DIM
RET_CHUNK = 128
D_FF = 5504
IN_COLS = 3 * ATTN_WIDTH + 4 * RET_WIDTH
ROPE_THETA = 10000.0
NORM_EPS = 1e-6
NEG_INF = -1e30
LOG2_E = 1.4426950408889634

V7X_LANES = 128
V7X_BF16_SUBLANES = 16
V7X_SCOPED_VMEM_BYTES = 56 * 1024 * 1024
FF_TILE = 512
D_FF_PAD = -(-D_FF // FF_TILE) * FF_TILE
MIX_ROWS = 512
FFN_ROWS = 1024
PROJ_ROWS = 512
CAST_TILE_ELEMS = 1024 * 1024
SEG = 1024
PROJ_CHUNK = RET_DIM
ATTN_QBLK = 128
ATTN_KWIN = 256
ATTN_SUPER = ATTN_QBLK * DILATIONS[-1]
BY4 = DILATIONS[1]
ATTN_UNROLL = 16
RET_ROWS = 1024
MIB = 1024 * 1024


def _params(semantics, vmem_bytes, claim_all=False):
    if claim_all:
        assert vmem_bytes <= V7X_SCOPED_VMEM_BYTES
        vmem_bytes = V7X_SCOPED_VMEM_BYTES
    return pltpu.CompilerParams(dimension_semantics=semantics, vmem_limit_bytes=int(vmem_bytes))


def _rms_norm_rows(x, gain):
    return x * lax.rsqrt(jnp.mean(x * x, axis=-1, keepdims=True) + NORM_EPS) * gain


def _cast_tile(w_ref, o_ref, row_tile, col_tile, rows, cols):
    tr, tc = o_ref.shape
    w = w_ref[...]
    if rows % tr or cols % tc:
        r = lax.broadcasted_iota(jnp.int32, (tr, tc), 0) + row_tile * tr
        c = lax.broadcasted_iota(jnp.int32, (tr, tc), 1) + col_tile * tc
        w = jnp.where((r < rows) & (c < cols), w, 0.0)
    o_ref[...] = w.astype(o_ref.dtype)


def _cast_kernel(w_ref, o_ref, *, rows, cols):
    _cast_tile(w_ref, o_ref, pl.program_id(0), pl.program_id(1), rows, cols)


class _CarriedCast(NamedTuple):
    w: jax.Array
    layer: int
    rows_pad: int
    cols_pad: int

    def tiling(self, n_row_blocks, n_col_steps):
        tr = self.rows_pad // n_row_blocks
        n_col_tiles = max(n for n in range(1, n_col_steps + 1)
                          if self.cols_pad % n == 0 and (self.cols_pad // n) % V7X_LANES == 0)
        assert self.rows_pad % n_row_blocks == 0 and tr % V7X_BF16_SUBLANES == 0
        return tr, self.cols_pad // n_col_tiles, n_col_tiles

    def specs(self, n_row_blocks, n_col_steps):
        tr, tc, n_col_tiles = self.tiling(n_row_blocks, n_col_steps)
        layer = self.layer
        col = lambda f: jnp.minimum(f, n_col_tiles - 1)
        return (pl.BlockSpec((None, tr, tc), lambda i, f: (layer, i, col(f))),
                pl.BlockSpec((tr, tc), lambda i, f: (i, col(f))),
                jax.ShapeDtypeStruct((self.rows_pad, self.cols_pad), BF16))


def _run_carried_casts(extents, in_refs, out_refs, i, f):
    for (rows, cols, n_col_tiles), w_ref, o_ref in zip(extents, in_refs, out_refs):
        _cast_tile(w_ref, o_ref, i, jnp.minimum(f, n_col_tiles - 1), rows, cols)


def _cast_layer(w, layer, rows_pad, cols_pad):
    _, rows, cols = w.shape
    tc = next(c for c in (2048, 1024, 512) if cols_pad % c == 0)
    tr = CAST_TILE_ELEMS // tc
    assert rows_pad % tr == 0
    return pl.pallas_call(
        functools.partial(_cast_kernel, rows=rows, cols=cols),
        grid=(rows_pad // tr, cols_pad // tc),
        in_specs=[pl.BlockSpec((None, tr, tc), lambda i, j: (layer, i, j))],
        out_specs=pl.BlockSpec((tr, tc), lambda i, j: (i, j)),
        out_shape=jax.ShapeDtypeStruct((rows_pad, cols_pad), BF16),
        compiler_params=_params(("parallel", "parallel"), 2 * tr * tc * (4 + 2) + 4 * MIB, claim_all=True),
        name="cast",
    )(w)


def _ffn_kernel(x_ref, g_ref, wg_ref, wu_ref, wd_ref, *refs, cast_extents):
    n_casts = len(cast_extents)
    cast_in, o_ref, cast_out, xn_ref = refs[:n_casts], refs[n_casts], refs[n_casts + 1:-1], refs[-1]
    i, f = pl.program_id(0), pl.program_id(1)
    last = pl.num_programs(1) - 1
    tm = x_ref.shape[0]

    def step(rows, first, final):
        if first:
            xn_ref[rows, :] = _rms_norm_rows(x_ref[rows, :], g_ref[...]).astype(BF16)
        xn = xn_ref[rows, :]
        gate = jnp.dot(xn, wg_ref[...], preferred_element_type=F32)
        up = jnp.dot(xn, wu_ref[...], preferred_element_type=F32)
        h = (jax.nn.silu(gate) * up).astype(BF16)
        acc = jnp.dot(h, wd_ref[...], preferred_element_type=F32)
        if not first:
            acc = o_ref[rows, :] + acc
        o_ref[rows, :] = x_ref[rows, :] + 0.5 * acc if final else acc

    halves = (slice(0, tm // 2), slice(tm // 2, tm))

    @pl.when(f == 0)
    def _():
        _run_carried_casts(cast_extents, cast_in, cast_out, i, f)
        for rows in halves:
            step(rows, True, False)

    @pl.when((f > 0) & (f < last))
    def _():
        _run_carried_casts(cast_extents, cast_in, cast_out, i, f)
        step(slice(0, tm), False, False)

    @pl.when(f == last)
    def _():
        _run_carried_casts(cast_extents, cast_in, cast_out, i, f)
        for rows in halves:
            step(rows, False, True)


def _ffn(x, gain, wg, wu, wd, casts=()):
    T, D = x.shape
    tm, tf = FFN_ROWS, FF_TILE
    grid = (T // tm, D_FF_PAD // tf)
    cast_specs = [c.specs(*grid) for c in casts]
    cast_extents = tuple(c.w.shape[1:] + (c.tiling(*grid)[2],) for c in casts)
    cast_vmem = sum(2 * tr * tc * (4 + 2) for tr, tc, _ in (c.tiling(*grid) for c in casts))
    vmem = 2 * (2 * tm * D * 4) + tm * D * 2 + 2 * 3 * D * tf * 2 + 3 * tm * tf * 4 + cast_vmem + 2 * MIB
    y, *cast_out = pl.pallas_call(
        functools.partial(_ffn_kernel, cast_extents=cast_extents),
        grid=grid,
        in_specs=[
            pl.BlockSpec((tm, D), lambda i, f: (i, 0)),
            pl.BlockSpec((1, D), lambda i, f: (0, 0)),
            pl.BlockSpec((D, tf), lambda i, f: (0, f)),
            pl.BlockSpec((D, tf), lambda i, f: (0, f)),
            pl.BlockSpec((tf, D), lambda i, f: (f, 0)),
        ] + [s[0] for s in cast_specs],
        out_specs=[pl.BlockSpec((tm, D), lambda i, f: (i, 0))] + [s[1] for s in cast_specs],
        out_shape=[jax.ShapeDtypeStruct((T, D), F32)] + [s[2] for s in cast_specs],
        scratch_shapes=[pltpu.VMEM((tm, D), BF16)],
        compiler_params=_params(("parallel", "arbitrary"), vmem),
        name="ffn",
    )(x, gain, wg, wu, wd, *[c.w for c in casts])
    return y, cast_out


def _proj_kernel(x_ref, g_ref, w_ref, cr_ref, sr_ref, o_ref, xn_ref):
    n = pl.program_id(1)

    tm = x_ref.shape[0]

    def by_chunks(epilogue, rows=slice(None)):
        xn = xn_ref[rows, :]
        for j in range(SEG // PROJ_CHUNK):
            c0 = j * PROJ_CHUNK
            y = jnp.dot(xn, w_ref[:, c0:c0 + PROJ_CHUNK], preferred_element_type=F32)
            epilogue(y, c0, rows)

    def ret_qk(scale):
        def epilogue(y, c0, rows):
            cos, sin = cr_ref[rows, :], sr_ref[rows, :]
            half = RET_DIM // 2
            lo, hi = y[:, :half], y[:, half:]
            o_ref[rows, c0:c0 + half] = (lo * cos[:, :half] + hi * sin[:, :half]) * scale
            o_ref[rows, c0 + half:c0 + RET_DIM] = (hi * cos[:, half:] + lo * sin[:, half:]) * scale
        by_chunks(epilogue)

    def plain(rows=slice(None)):
        o_ref[rows, :] = jnp.dot(xn_ref[rows, :], w_ref[...], preferred_element_type=F32)

    @pl.when(n == 0)
    def _():
        for rows in (slice(0, tm // 2), slice(tm // 2, tm)):
            xn_ref[rows, :] = _rms_norm_rows(x_ref[rows, :], g_ref[...]).astype(BF16)
            plain(rows)

    @pl.when(n == 3)
    def _():
        ret_qk(1.0)

    @pl.when(n == 4)
    def _():
        ret_qk(RET_DIM ** -0.5)

    @pl.when(((n > 0) & (n < 3)) | (n >= 5))
    def _():
        plain()


def _rope_tables(seq, head_dim):
    half = head_dim // 2
    inv_freq = ROPE_THETA ** (-jnp.arange(half, dtype=F32) / half)
    ang = jnp.arange(seq, dtype=F32)[:, None] * inv_freq[None, :]
    cos, sin = jnp.cos(ang), jnp.sin(ang)
    return jnp.concatenate([cos, cos], axis=-1), jnp.concatenate([-sin, sin], axis=-1)


def _proj(x, gain, w_in, ret_tables, seq):
    T, D = x.shape
    tm = PROJ_ROWS
    nseq = seq // tm
    cr, sr = ret_tables
    pos = lambda i, n: (i % nseq, 0)
    const = lambda i, n: (0, 0)
    vmem = 2 * tm * D * 4 + tm * D * 2 + 2 * D * SEG * 2 + 2 * tm * SEG * 4 + 2 * 2 * tm * RET_DIM * 4 \
        + 6 * tm * PROJ_CHUNK * 4 + 4 * MIB
    return pl.pallas_call(
        _proj_kernel,
        grid=(T // tm, IN_COLS // SEG),
        in_specs=[
            pl.BlockSpec((tm, D), lambda i, n: (i, 0)),
            pl.BlockSpec((1, D), const),
            pl.BlockSpec((D, SEG), lambda i, n: (0, n)),
            pl.BlockSpec((tm, RET_DIM), pos),
            pl.BlockSpec((tm, RET_DIM), pos),
        ],
        out_specs=pl.BlockSpec((tm, SEG), lambda i, n: (i, n)),
        out_shape=jax.ShapeDtypeStruct((T, IN_COLS), F32),
        scratch_shapes=[pltpu.VMEM((tm, D), BF16)],
        compiler_params=_params(("parallel", "arbitrary"), vmem),
        name="proj",
    )(x, gain, w_in, cr, sr)


def _attn_kernel(qraw_ref, kraw_ref, v_ref, qg_ref, kg_ref, cos_ref, sin_ref, o_ref,
                 q_ref, k_ref, po_ref, pl_ref, bias_ref, *by4_refs, seq):
    scale = ATTN_HEAD_DIM ** -0.5
    exp2_scale = scale * LOG2_E
    blocks_per_super = ATTN_SUPER // ATTN_QBLK

    rel = (lax.broadcasted_iota(jnp.int32, (ATTN_QBLK, ATTN_KWIN), 0)
           - lax.broadcasted_iota(jnp.int32, (ATTN_QBLK, ATTN_KWIN), 1))
    for case in range(3):
        bias_ref[case] = jnp.where(jnp.abs(rel + case * ATTN_HALF) <= ATTN_HALF, 0.0, NEG_INF)

    ones = jnp.ones((ATTN_HEAD_DIM, ATTN_HEAD_DIM), BF16)

    def row_sums(t):
        hi = t.astype(BF16)
        lo = (t - hi.astype(F32)).astype(BF16)
        return jnp.dot(hi, ones, preferred_element_type=F32) + jnp.dot(lo, ones, preferred_element_type=F32)

    def qk_norm_rope(i, c):
        rows = pl.ds(pl.multiple_of(i * ATTN_KWIN, ATTN_KWIN), ATTN_KWIN)
        cos, sin = cos_ref[rows, :], sin_ref[rows, :]
        for raw_ref, gain_ref, dst_ref in ((qraw_ref, qg_ref, q_ref), (kraw_ref, kg_ref, k_ref)):
            y = raw_ref[rows, :]
            mean_sq = row_sums(y * y) * (1.0 / ATTN_HEAD_DIM)
            yn = y * lax.rsqrt(mean_sq + NORM_EPS) * gain_ref[...]
            dst_ref[rows, :] = yn * cos + pltpu.roll(yn, ATTN_HEAD_DIM // 2, 1) * sin
        return c

    by4_ref = by4_refs[0] if by4_refs else None
    n_chunks = seq // ATTN_KWIN

    def regroup(chunk):
        c0 = pl.multiple_of(chunk * ATTN_KWIN, ATTN_KWIN)
        n = ATTN_KWIN // BY4
        dst = pl.ds(pl.multiple_of(chunk * n, n), n)
        for t, src_ref in enumerate((q_ref, k_ref, v_ref)):
            for r in range(BY4):
                by4_ref[t, r, dst, :] = src_ref[pl.ds(c0 + r, n, stride=BY4), :]

    if by4_ref is None:
        lax.fori_loop(0, n_chunks, qk_norm_rope, 0, unroll=4)
    else:
        def regroup_then_norm(t, c):
            regroup(t - 1)
            return qk_norm_rope(t, c)

        qk_norm_rope(jnp.int32(0), 0)
        lax.fori_loop(1, n_chunks, regroup_then_norm, 0, unroll=4)
        regroup(jnp.int32(n_chunks - 1))

    def super_block(sb, carry):
        base = pl.multiple_of(sb * ATTN_SUPER, ATTN_SUPER)

        def blocks(u, c):
            for p, d in enumerate(DILATIONS):
                sub_len = seq // d
                per_res = blocks_per_super // d
                r = u % d
                jl = u // d
                jb = sb * per_res + jl
                q0 = jb * ATTN_QBLK
                k0 = jnp.clip(q0 - ATTN_HALF, 0, sub_len - ATTN_KWIN)
                if d == 1:
                    qs, ks = pl.ds(q0, ATTN_QBLK), pl.ds(k0, ATTN_KWIN)
                    os = pl.ds(jl * ATTN_QBLK, ATTN_QBLK)
                else:
                    qs = pl.ds(r + d * q0, ATTN_QBLK, stride=d)
                    ks = pl.ds(r + d * k0, ATTN_KWIN, stride=d)
                    os = pl.ds(r + d * jl * ATTN_QBLK, ATTN_QBLK, stride=d)
                if by4_ref is not None and d % BY4 == 0:
                    sub, lane_r, off = d // BY4, r % BY4, r // BY4
                    if sub == 1:
                        qs4, ks4 = pl.ds(q0, ATTN_QBLK), pl.ds(k0, ATTN_KWIN)
                    else:
                        qs4 = pl.ds(off + sub * q0, ATTN_QBLK, stride=sub)
                        ks4 = pl.ds(off + sub * k0, ATTN_KWIN, stride=sub)
                    q = by4_ref[0, lane_r, qs4, :].astype(BF16)
                    k = by4_ref[1, lane_r, ks4, :].astype(BF16)
                    v = by4_ref[2, lane_r, ks4, :].astype(BF16)
                else:
                    q = q_ref[qs, :].astype(BF16)
                    k = k_ref[ks, :].astype(BF16)
                    v = v_ref[ks, :].astype(BF16)
                s = lax.dot_general(q, k, (((1,), (1,)), ((), ())), preferred_element_type=F32)
                s = s + bias_ref[(q0 - k0) // ATTN_HALF]
                m = jnp.max(s, axis=-1, keepdims=True)
                e = jnp.exp2((s - m) * exp2_scale)
                den = jnp.sum(e, axis=-1, keepdims=True)
                out = jnp.dot(e.astype(BF16), v, preferred_element_type=F32) / den
                po_ref[p, os, :] = out
                pl_ref[p, os, :] = jnp.broadcast_to(m * exp2_scale + jnp.log2(den), (ATTN_QBLK, ATTN_HEAD_DIM))
            return c

        lax.fori_loop(0, blocks_per_super, blocks, 0, unroll=ATTN_UNROLL)

        def merge(i, c):
            rows = pl.ds(pl.multiple_of(i * ATTN_QBLK, ATTN_QBLK), ATTN_QBLK)
            lse = [pl_ref[p, rows, :] for p in range(len(DILATIONS))]
            top = functools.reduce(jnp.maximum, lse)
            w = [jnp.exp2(l - top) for l in lse]
            tot = functools.reduce(jnp.add, w)
            acc = functools.reduce(jnp.add, [w[p] * po_ref[p, rows, :] for p in range(len(DILATIONS))])
            o_ref[pl.ds(base + i * ATTN_QBLK, ATTN_QBLK), :] = (acc / tot).astype(o_ref.dtype)
            return c

        lax.fori_loop(0, blocks_per_super, merge, 0)
        return carry

    lax.fori_loop(0, seq // ATTN_SUPER, super_block, 0)


def _attn(p3, q_gain, k_gain, tables):
    B, S, _ = p3.shape
    assert S % ATTN_SUPER == 0 and S // DILATIONS[-1] >= ATTN_KWIN
    blk = (None, S, ATTN_HEAD_DIM)
    const = lambda b, h: (0, 0)
    once = dict(pipeline_mode=pl.Buffered(1))
    vmem = 2 * 3 * S * 128 * 4 + 2 * S * 128 * 2 + 4 * S * 128 * 4 + 2 * len(DILATIONS) * ATTN_SUPER * 128 * 4 \
        + 4 * MIB
    by4_bytes = 3 * S * ATTN_HEAD_DIM * 4
    by4 = [pltpu.VMEM((3, BY4, S // BY4, ATTN_HEAD_DIM), F32)] if vmem + by4_bytes <= V7X_SCOPED_VMEM_BYTES else []
    vmem += by4_bytes * len(by4)
    return pl.pallas_call(
        functools.partial(_attn_kernel, seq=S),
        grid=(B, ATTN_HEADS),
        in_specs=[
            pl.BlockSpec(blk, lambda b, h: (b, 0, h)),
            pl.BlockSpec(blk, lambda b, h: (b, 0, ATTN_HEADS + h)),
            pl.BlockSpec(blk, lambda b, h: (b, 0, 2 * ATTN_HEADS + h)),
            pl.BlockSpec((1, ATTN_HEAD_DIM), const),
            pl.BlockSpec((1, ATTN_HEAD_DIM), const),
            pl.BlockSpec((S, ATTN_HEAD_DIM), const, **once),
            pl.BlockSpec((S, ATTN_HEAD_DIM), const, **once),
        ],
        out_specs=pl.BlockSpec(blk, lambda b, h: (b, 0, h)),
        out_shape=jax.ShapeDtypeStruct((B, S, ATTN_WIDTH), BF16),
        scratch_shapes=[pltpu.VMEM((S, ATTN_HEAD_DIM), F32),
                        pltpu.VMEM((S, ATTN_HEAD_DIM), F32),
                        pltpu.VMEM((len(DILATIONS), ATTN_SUPER, ATTN_HEAD_DIM), F32),
                        pltpu.VMEM((len(DILATIONS), ATTN_SUPER, ATTN_HEAD_DIM), F32),
                        pltpu.VMEM((3, ATTN_QBLK, ATTN_KWIN), F32)] + by4,
        compiler_params=_params(("parallel", "parallel"), vmem, claim_all=True),
        name="attn",
    )(p3, p3, p3, q_gain, k_gain, *tables)


def _decay_tables(lg, backward):
    C = RET_CHUNK
    i = lax.broadcasted_iota(jnp.int32, (C, C), 0)
    j = lax.broadcasted_iota(jnp.int32, (C, C), 1)
    diff = (j - i) if backward else (i - j)
    mask = jnp.where(diff >= 0, jnp.exp(lg * jnp.maximum(diff, 0).astype(F32)), 0.0)
    idx = lax.broadcasted_iota(jnp.int32, (C, 1), 0).astype(F32)
    if backward:
        k_pow, q_pow = idx, C - idx
    else:
        k_pow, q_pow = C - 1 - idx, idx + 1
    chunk = jnp.exp(jnp.full((1, RET_DIM), C, F32) * lg)
    return mask, jnp.exp(lg * k_pow), jnp.exp(lg * q_pow), chunk


def _ret_cross(q, k, vb, state, k_scale, q_scale, chunk_decay):
    cross = jnp.dot((q * q_scale).astype(BF16), state.astype(BF16), preferred_element_type=F32)
    kv = lax.dot_general((k * k_scale).astype(BF16), vb, (((0,), (0,)), ((), ())), preferred_element_type=F32)
    return cross, chunk_decay * state + kv


def _ret_bwd_rows(lg, q_ref, k_ref, v_ref, o_ref, state_ref, row0):
    _, k_scale, q_scale, chunk_decay = _decay_tables(lg, backward=True)
    state = state_ref[...]
    for c in reversed(range(RET_ROWS // RET_CHUNK)):
        rows = slice(row0 + c * RET_CHUNK, row0 + (c + 1) * RET_CHUNK)
        cross, state = _ret_cross(q_ref[rows, :], k_ref[rows, :], v_ref[rows, :].astype(BF16), state,
                                  k_scale, q_scale, chunk_decay)
        o_ref[rows, :] = cross
    state_ref[...] = state


def _ret_fwd_rows(lg_fwd, lg_bwd, q_ref, k_ref, v_ref, gate_ref, bwd_ref, gain_ref, o_ref, state_ref, row0):
    mask_f, k_scale, q_scale, chunk_decay = _decay_tables(lg_fwd, backward=False)
    mask = mask_f + _decay_tables(lg_bwd, backward=True)[0]
    state = state_ref[...]
    for c in range(RET_ROWS // RET_CHUNK):
        rows = slice(row0 + c * RET_CHUNK, row0 + (c + 1) * RET_CHUNK)
        q, k, vb = q_ref[rows, :], k_ref[rows, :], v_ref[rows, :].astype(BF16)
        s = lax.dot_general(q.astype(BF16), k.astype(BF16), (((1,), (1,)), ((), ())), preferred_element_type=F32)
        inner = jnp.dot((s * mask).astype(BF16), vb, preferred_element_type=F32)
        cross, state = _ret_cross(q, k, vb, state, k_scale, q_scale, chunk_decay)
        both = _rms_norm_rows(inner + cross + bwd_ref[rows, :], gain_ref[...])
        o_ref[rows, :] = (jax.nn.silu(gate_ref[rows, :]) * both).astype(o_ref.dtype)
    state_ref[...] = state


def _ret_bwd_kernel(ld_ref, q_ref, k_ref, v_ref, o_ref, state_ref):
    h, t = pl.program_id(1), pl.program_id(2)

    @pl.when(t == 0)
    def _():
        state_ref[...] = jnp.zeros_like(state_ref)

    _ret_bwd_rows(ld_ref[h], q_ref, k_ref, v_ref, o_ref, state_ref, 0)


def _ret_fwd_kernel(ldf_ref, ldb_ref, q_ref, k_ref, v_ref, gate_ref, bwd_ref, gain_ref, o_ref, state_ref):
    h, t = pl.program_id(1), pl.program_id(2)

    @pl.when(t == 0)
    def _():
        state_ref[...] = jnp.zeros_like(state_ref)

    _ret_fwd_rows(ldf_ref[h], ldb_ref[h], q_ref, k_ref, v_ref, gate_ref, bwd_ref, gain_ref, o_ref, state_ref, 0)


def _ret_fused_kernel(ldf_ref, ldb_ref, q_ref, k_ref, v_ref, gate_ref, gain_ref, o_ref, bwd_ref, state_ref, *, seq):
    h = pl.program_id(1)
    blocks = range(0, seq, RET_ROWS)
    state_ref[...] = jnp.zeros_like(state_ref)
    for row0 in reversed(blocks):
        _ret_bwd_rows(ldb_ref[h], q_ref, k_ref, v_ref, bwd_ref, state_ref, row0)
    state_ref[...] = jnp.zeros_like(state_ref)
    for row0 in blocks:
        _ret_fwd_rows(ldf_ref[h], ldb_ref[h], q_ref, k_ref, v_ref, gate_ref, bwd_ref, gain_ref, o_ref, state_ref, row0)


def _ret_specs(n_t, reverse):
    tmap = (lambda t: n_t - 1 - t) if reverse else (lambda t: t)
    blk = (None, RET_ROWS, RET_DIM)
    seg0 = 3 * ATTN_WIDTH // RET_DIM

    def col(seg):
        return pl.BlockSpec(blk, lambda b, h, t: (b, tmap(t), seg0 + seg * RET_HEADS + h))

    return blk, tmap, col


def _ret_bwd(p3, log_decay):
    B, S, _ = p3.shape
    n_t = S // RET_ROWS
    blk, tmap, col = _ret_specs(n_t, reverse=True)
    vmem = 2 * 4 * RET_ROWS * RET_DIM * 4 + RET_DIM * RET_DIM * 4 + 8 * MIB
    return pl.pallas_call(
        _ret_bwd_kernel,
        grid=(B, RET_HEADS, n_t),
        in_specs=[pl.BlockSpec(memory_space=pltpu.SMEM), col(0), col(1), col(2)],
        out_specs=pl.BlockSpec(blk, lambda b, h, t: (b, tmap(t), h)),
        out_shape=jax.ShapeDtypeStruct((B, S, RET_WIDTH), F32),
        scratch_shapes=[pltpu.VMEM((RET_DIM, RET_DIM), F32)],
        compiler_params=_params(("parallel", "parallel", "arbitrary"), vmem, claim_all=True),
        name="ret_bwd",
    )(log_decay, p3, p3, p3)


def _ret_fwd(p3, bwd, log_decay_fwd, log_decay_bwd, out_gain):
    B, S, _ = p3.shape
    n_t = S // RET_ROWS
    blk, tmap, col = _ret_specs(n_t, reverse=False)
    vmem = 2 * 6 * RET_ROWS * RET_DIM * 4 + RET_DIM * RET_DIM * 4 + 8 * MIB
    smem = pl.BlockSpec(memory_space=pltpu.SMEM)
    return pl.pallas_call(
        _ret_fwd_kernel,
        grid=(B, RET_HEADS, n_t),
        in_specs=[smem, smem, col(0), col(1), col(2), col(3),
                  pl.BlockSpec(blk, lambda b, h, t: (b, t, h)),
                  pl.BlockSpec((None, 1, RET_DIM), lambda b, h, t: (h, 0, 0))],
        out_specs=pl.BlockSpec(blk, lambda b, h, t: (b, t, h)),
        out_shape=jax.ShapeDtypeStruct((B, S, RET_WIDTH), BF16),
        scratch_shapes=[pltpu.VMEM((RET_DIM, RET_DIM), F32)],
        compiler_params=_params(("parallel", "parallel", "arbitrary"), vmem, claim_all=True),
        name="ret_fwd",
    )(log_decay_fwd, log_decay_bwd, p3, p3, p3, p3, bwd, out_gain)


def _retention(p3, log_decay_fwd, log_decay_bwd, out_gain):
    B, S, _ = p3.shape
    seq_bytes = S * RET_DIM * 4
    fused_vmem = 2 * 4 * seq_bytes + seq_bytes + 2 * seq_bytes // 2 + RET_DIM * RET_DIM * 4 + 6 * MIB
    if fused_vmem > V7X_SCOPED_VMEM_BYTES:
        bwd = _ret_bwd(p3, log_decay_bwd)
        return _ret_fwd(p3, bwd, log_decay_fwd, log_decay_bwd, out_gain)
    blk = (None, S, RET_DIM)
    seg0 = 3 * ATTN_WIDTH // RET_DIM
    col = lambda seg: pl.BlockSpec(blk, lambda b, h: (b, 0, seg0 + seg * RET_HEADS + h))
    smem = pl.BlockSpec(memory_space=pltpu.SMEM)
    return pl.pallas_call(
        functools.partial(_ret_fused_kernel, seq=S),
        grid=(B, RET_HEADS),
        in_specs=[smem, smem, col(0), col(1), col(2), col(3),
                  pl.BlockSpec((None, 1, RET_DIM), lambda b, h: (h, 0, 0))],
        out_specs=pl.BlockSpec(blk, lambda b, h: (b, 0, h)),
        out_shape=jax.ShapeDtypeStruct((B, S, RET_WIDTH), BF16),
        scratch_shapes=[pltpu.VMEM((S, RET_DIM), F32), pltpu.VMEM((RET_DIM, RET_DIM), F32)],
        compiler_params=_params(("parallel", "parallel"), fused_vmem, claim_all=True),
        name="ret_fused",
    )(log_decay_fwd, log_decay_bwd, p3, p3, p3, p3, out_gain)


def _mix_out_kernel(x_ref, a_ref, r_ref, w_ref, o_ref):
    acc = jnp.dot(a_ref[...], w_ref[:ATTN_WIDTH, :], preferred_element_type=F32)
    acc += jnp.dot(r_ref[...], w_ref[ATTN_WIDTH:, :], preferred_element_type=F32)
    o_ref[...] = x_ref[...] + acc


def _mix_out(x, a, r, w_out):
    T, D = x.shape
    tm = MIX_ROWS
    vmem = 2 * (2 * tm * D * 4 + 2 * tm * SEG * 2) + D * D * 2 + tm * D * 4 + 4 * MIB
    return pl.pallas_call(
        _mix_out_kernel,
        grid=(T // tm,),
        in_specs=[
            pl.BlockSpec((tm, D), lambda i: (i, 0)),
            pl.BlockSpec((tm, ATTN_WIDTH), lambda i: (i, 0)),
            pl.BlockSpec((tm, RET_WIDTH), lambda i: (i, 0)),
            pl.BlockSpec((D, D), lambda i: (0, 0), pipeline_mode=pl.Buffered(1)),
        ],
        out_specs=pl.BlockSpec((tm, D), lambda i: (i, 0)),
        out_shape=jax.ShapeDtypeStruct((T, D), F32),
        compiler_params=_params(("parallel",), vmem, claim_all=True),
        name="mix_out",
    )(x, a, r, w_out)


_MIXER_WEIGHTS = ("w_in", "w_out")
_FFN1_WEIGHTS = ("ffn1_wg", "ffn1_wu", "ffn1_wd")
_FFN2_WEIGHTS = ("ffn2_wg", "ffn2_wu", "ffn2_wd")
_WEIGHT_PADS = {"w_in": (D_MODEL, IN_COLS), "w_out": (D_MODEL, D_MODEL),
                "ffn1_wg": (D_MODEL, D_FF_PAD), "ffn1_wu": (D_MODEL, D_FF_PAD), "ffn1_wd": (D_FF_PAD, D_MODEL),
                "ffn2_wg": (D_MODEL, D_FF_PAD), "ffn2_wu": (D_MODEL, D_FF_PAD), "ffn2_wd": (D_FF_PAD, D_MODEL)}


def _trunk(x3, layers, f32_weights, bf16_weights):
    B, S, D = x3.shape
    T = B * S
    x = x3.reshape(T, D)
    attn_tables, ret_tables = _rope_tables(S, ATTN_HEAD_DIM), _rope_tables(S, RET_DIM)

    def ffn_carrying(x, gain, own, layer, carried, carried_layer):
        names = [n for n in carried if carried_layer < len(layers) and (n, carried_layer) not in bf16_weights]
        casts = [_CarriedCast(f32_weights[n], carried_layer, *_WEIGHT_PADS[n]) for n in names]
        y, cast = _ffn(x, gain, *[bf16_weights[(n, layer)] for n in own], casts=casts)
        bf16_weights.update({(n, carried_layer): w for n, w in zip(names, cast)})
        return y

    for l, lw in enumerate(layers):
        x = ffn_carrying(x, lw["ffn1_norm"], _FFN1_WEIGHTS, l, _MIXER_WEIGHTS + _FFN2_WEIGHTS, l)
        proj = _proj(x, lw["mix_norm"], bf16_weights[("w_in", l)], ret_tables, S)
        p3 = proj.reshape(B, S, IN_COLS)
        a = _attn(p3, lw["q_gain"], lw["k_gain"], attn_tables)
        r = _retention(p3, lw["ld_fwd"], lw["ld_bwd"], lw["ret_gain"])
        x = _mix_out(x, a.reshape(T, ATTN_WIDTH), r.reshape(T, RET_WIDTH), bf16_weights[("w_out", l)])
        x = ffn_carrying(x, lw["ffn2_norm"], _FFN2_WEIGHTS, l, _FFN1_WEIGHTS, l + 1)
    return x.reshape(B, S, D)


def kernel(x_prompt, x_sample, ffn1_norm, ffn1_w_gate, ffn1_w_up, ffn1_w_down, mix_norm, w_in, attn_q_norm, attn_k_norm, ret_log_decay_fwd, ret_log_decay_bwd, ret_out_norm, w_out, ffn2_norm, ffn2_w_gate, ffn2_w_up, ffn2_w_down):
    depth = w_in.shape[0]
    layers = []
    for l in range(depth):
        layers.append({
            "ffn1_norm": ffn1_norm[l][None, :].astype(F32),
            "mix_norm": mix_norm[l][None, :].astype(F32),
            "q_gain": attn_q_norm[l][None, :].astype(F32),
            "k_gain": attn_k_norm[l][None, :].astype(F32),
            "ld_fwd": ret_log_decay_fwd[l].astype(F32),
            "ld_bwd": ret_log_decay_bwd[l].astype(F32),
            "ret_gain": ret_out_norm[l][:, None, :].astype(F32),
            "ffn2_norm": ffn2_norm[l][None, :].astype(F32),
        })
    f32_weights = {"w_in": w_in, "w_out": w_out,
                   "ffn1_wg": ffn1_w_gate, "ffn1_wu": ffn1_w_up, "ffn1_wd": ffn1_w_down,
                   "ffn2_wg": ffn2_w_gate, "ffn2_wu": ffn2_w_up, "ffn2_wd": ffn2_w_down}
    bf16_weights = {(n, 0): _cast_layer(f32_weights[n], 0, *_WEIGHT_PADS[n]) for n in _FFN1_WEIGHTS}
    y_sample = _trunk(x_sample, layers, f32_weights, bf16_weights)
    y_prompt = _trunk(x_prompt, layers, f32_weights, bf16_weights)
    return (y_prompt, y_sample)
```

```python
import functools
from typing import NamedTuple

import jax
import jax.numpy as jnp
from jax import lax
from jax.experimental import pallas as pl
from jax.experimental.pallas import tpu as pltpu

F32 = jnp.float32
BF16 = jnp.bfloat16

D_MODEL = 2048
ATTN_HEAD_DIM = 128
ATTN_HEADS = 8
ATTN_WIDTH = ATTN_HEADS * ATTN_HEAD_DIM
DILATIONS = (1, 4, 16)
ATTN_HALF = 64
RET_HEADS = 4
RET_DIM = 256
RET_WIDTH = RET_HEADS * RET_DIM
RET_CHUNK = 128
D_FF = 5504
IN_COLS = 3 * ATTN_WIDTH + 4 * RET_WIDTH
ROPE_THETA = 10000.0
NORM_EPS = 1e-6
NEG_INF = -1e30
LOG2_E = 1.4426950408889634

V7X_LANES = 128
V7X_BF16_SUBLANES = 16
V7X_SCOPED_VMEM_BYTES = 56 * 1024 * 1024
FF_TILE = 512
D_FF_PAD = -(-D_FF // FF_TILE) * FF_TILE
MIX_ROWS = 512
FFN_ROWS = 1024
PROJ_ROWS = 1024
CAST_TILE_ELEMS = 1024 * 1024
SEG = 1024
PROJ_CHUNK = RET_DIM
ATTN_QBLK = 128
ATTN_KWIN = 256
ATTN_SUPER = ATTN_QBLK * DILATIONS[-1]
BY4 = DILATIONS[1]
ATTN_UNROLL = 16
RET_ROWS = 1024
MIB = 1024 * 1024


def _params(semantics, vmem_bytes, claim_all=False):
    if claim_all:
        assert vmem_bytes <= V7X_SCOPED_VMEM_BYTES
        vmem_bytes = V7X_SCOPED_VMEM_BYTES
    return pltpu.CompilerParams(dimension_semantics=semantics, vmem_limit_bytes=int(vmem_bytes))


def _rms_norm_rows(x, gain):
    return x * lax.rsqrt(jnp.mean(x * x, axis=-1, keepdims=True) + NORM_EPS) * gain


def _cast_tile(w_ref, o_ref, row_tile, col_tile, rows, cols):
    tr, tc = o_ref.shape
    w = w_ref[...]
    if rows % tr or cols % tc:
        r = lax.broadcasted_iota(jnp.int32, (tr, tc), 0) + row_tile * tr
        c = lax.broadcasted_iota(jnp.int32, (tr, tc), 1) + col_tile * tc
        w = jnp.where((r < rows) & (c < cols), w, 0.0)
    o_ref[...] = w.astype(o_ref.dtype)


def _cast_kernel(w_ref, o_ref, *, rows, cols):
    _cast_tile(w_ref, o_ref, pl.program_id(0), pl.program_id(1), rows, cols)


class _CarriedCast(NamedTuple):
    w: jax.Array
    layer: int
    rows_pad: int
    cols_pad: int

    def tiling(self, n_row_blocks, n_col_steps):
        tr = self.rows_pad // n_row_blocks
        n_col_tiles = max(n for n in range(1, n_col_steps + 1)
                          if self.cols_pad % n == 0 and (self.cols_pad // n) % V7X_LANES == 0)
        assert self.rows_pad % n_row_blocks == 0 and tr % V7X_BF16_SUBLANES == 0
        return tr, self.cols_pad // n_col_tiles, n_col_tiles

    def specs(self, n_row_blocks, n_col_steps):
        tr, tc, n_col_tiles = self.tiling(n_row_blocks, n_col_steps)
        layer = self.layer
        col = lambda f: jnp.minimum(f, n_col_tiles - 1)
        return (pl.BlockSpec((None, tr, tc), lambda i, f: (layer, i, col(f))),
                pl.BlockSpec((tr, tc), lambda i, f: (i, col(f))),
                jax.ShapeDtypeStruct((self.rows_pad, self.cols_pad), BF16))


def _run_carried_casts(extents, in_refs, out_refs, i, f):
    for (rows, cols, n_col_tiles), w_ref, o_ref in zip(extents, in_refs, out_refs):
        _cast_tile(w_ref, o_ref, i, jnp.minimum(f, n_col_tiles - 1), rows, cols)


def _cast_layer(w, layer, rows_pad, cols_pad):
    _, rows, cols = w.shape
    tc = next(c for c in (2048, 1024, 512) if cols_pad % c == 0)
    tr = CAST_TILE_ELEMS // tc
    assert rows_pad % tr == 0
    return pl.pallas_call(
        functools.partial(_cast_kernel, rows=rows, cols=cols),
        grid=(rows_pad // tr, cols_pad // tc),
        in_specs=[pl.BlockSpec((None, tr, tc), lambda i, j: (layer, i, j))],
        out_specs=pl.BlockSpec((tr, tc), lambda i, j: (i, j)),
        out_shape=jax.ShapeDtypeStruct((rows_pad, cols_pad), BF16),
        compiler_params=_params(("parallel", "parallel"), 2 * tr * tc * (4 + 2) + 4 * MIB, claim_all=True),
        name="cast",
    )(w)


def _ffn_kernel(x_ref, g_ref, wg_ref, wu_ref, wd_ref, *refs, cast_extents):
    n_casts = len(cast_extents)
    cast_in, o_ref, cast_out, xn_ref = refs[:n_casts], refs[n_casts], refs[n_casts + 1:-1], refs[-1]
    i, f = pl.program_id(0), pl.program_id(1)
    last = pl.num_programs(1) - 1
    tm = x_ref.shape[0]

    def step(rows, first, final):
        if first:
            xn_ref[rows, :] = _rms_norm_rows(x_ref[rows, :], g_ref[...]).astype(BF16)
        xn = xn_ref[rows, :]
        gate = jnp.dot(xn, wg_ref[...], preferred_element_type=F32)
        up = jnp.dot(xn, wu_ref[...], preferred_element_type=F32)
        h = (jax.nn.silu(gate) * up).astype(BF16)
        acc = jnp.dot(h, wd_ref[...], preferred_element_type=F32)
        if not first:
            acc = o_ref[rows, :] + acc
        o_ref[rows, :] = x_ref[rows, :] + 0.5 * acc if final else acc

    halves = (slice(0, tm // 2), slice(tm // 2, tm))

    @pl.when(f == 0)
    def _():
        _run_carried_casts(cast_extents, cast_in, cast_out, i, f)
        for rows in halves:
            step(rows, True, False)

    @pl.when((f > 0) & (f < last))
    def _():
        _run_carried_casts(cast_extents, cast_in, cast_out, i, f)
        step(slice(0, tm), False, False)

    @pl.when(f == last)
    def _():
        _run_carried_casts(cast_extents, cast_in, cast_out, i, f)
        for rows in halves:
            step(rows, False, True)


def _ffn(x, gain, wg, wu, wd, casts=()):
    T, D = x.shape
    tm, tf = FFN_ROWS, FF_TILE
    grid = (T // tm, D_FF_PAD // tf)
    cast_specs = [c.specs(*grid) for c in casts]
    cast_extents = tuple(c.w.shape[1:] + (c.tiling(*grid)[2],) for c in casts)
    cast_vmem = sum(2 * tr * tc * (4 + 2) for tr, tc, _ in (c.tiling(*grid) for c in casts))
    vmem = 2 * (2 * tm * D * 4) + tm * D * 2 + 2 * 3 * D * tf * 2 + 3 * tm * tf * 4 + cast_vmem + 2 * MIB
    y, *cast_out = pl.pallas_call(
        functools.partial(_ffn_kernel, cast_extents=cast_extents),
        grid=grid,
        in_specs=[
            pl.BlockSpec((tm, D), lambda i, f: (i, 0)),
            pl.BlockSpec((1, D), lambda i, f: (0, 0)),
            pl.BlockSpec((D, tf), lambda i, f: (0, f)),
            pl.BlockSpec((D, tf), lambda i, f: (0, f)),
            pl.BlockSpec((tf, D), lambda i, f: (f, 0)),
        ] + [s[0] for s in cast_specs],
        out_specs=[pl.BlockSpec((tm, D), lambda i, f: (i, 0))] + [s[1] for s in cast_specs],
        out_shape=[jax.ShapeDtypeStruct((T, D), F32)] + [s[2] for s in cast_specs],
        scratch_shapes=[pltpu.VMEM((tm, D), BF16)],
        compiler_params=_params(("parallel", "arbitrary"), vmem),
        name="ffn",
    )(x, gain, wg, wu, wd, *[c.w for c in casts])
    return y, cast_out


def _proj_kernel(x_ref, g_ref, w_ref, cr_ref, sr_ref, o_ref, xn_ref):
    n = pl.program_id(1)

    tm = x_ref.shape[0]

    def by_chunks(epilogue, rows=slice(None)):
        xn = xn_ref[rows, :]
        for j in range(SEG // PROJ_CHUNK):
            c0 = j * PROJ_CHUNK
            y = jnp.dot(xn, w_ref[:, c0:c0 + PROJ_CHUNK], preferred_element_type=F32)
            epilogue(y, c0, rows)

    def ret_qk(scale):
        def epilogue(y, c0, rows):
            cos, sin = cr_ref[rows, :], sr_ref[rows, :]
            half = RET_DIM // 2
            lo, hi = y[:, :half], y[:, half:]
            o_ref[rows, c0:c0 + half] = ((lo * cos[:, :half] + hi * sin[:, :half]) * scale).astype(o_ref.dtype)
            o_ref[rows, c0 + half:c0 + RET_DIM] = ((hi * cos[:, half:] + lo * sin[:, half:]) * scale).astype(o_ref.dtype)
        by_chunks(epilogue)

    def plain(rows=slice(None)):
        o_ref[rows, :] = jnp.dot(xn_ref[rows, :], w_ref[...], preferred_element_type=F32).astype(o_ref.dtype)

    @pl.when(n == 0)
    def _():
        for rows in (slice(0, tm // 2), slice(tm // 2, tm)):
            xn_ref[rows, :] = _rms_norm_rows(x_ref[rows, :], g_ref[...]).astype(BF16)
            plain(rows)

    @pl.when(n == 3)
    def _():
        ret_qk(1.0)

    @pl.when(n == 4)
    def _():
        ret_qk(RET_DIM ** -0.5)

    @pl.when(((n > 0) & (n < 3)) | (n >= 5))
    def _():
        plain()


def _rope_tables(seq, head_dim):
    half = head_dim // 2
    inv_freq = ROPE_THETA ** (-jnp.arange(half, dtype=F32) / half)
    ang = jnp.arange(seq, dtype=F32)[:, None] * inv_freq[None, :]
    cos, sin = jnp.cos(ang), jnp.sin(ang)
    return jnp.concatenate([cos, cos], axis=-1), jnp.concatenate([-sin, sin], axis=-1)


def _proj(x, gain, w_in, ret_tables, seq):
    T, D = x.shape
    tm = PROJ_ROWS
    nseq = seq // tm
    cr, sr = ret_tables
    pos = lambda i, n: (i % nseq, 0)
    const = lambda i, n: (0, 0)
    vmem = 2 * tm * D * 4 + tm * D * 2 + 2 * D * SEG * 2 + 2 * tm * SEG * 2 + 2 * 2 * tm * RET_DIM * 4 \
        + tm * SEG * 4 + 4 * MIB
    return pl.pallas_call(
        _proj_kernel,
        grid=(T // tm, IN_COLS // SEG),
        in_specs=[
            pl.BlockSpec((tm, D), lambda i, n: (i, 0)),
            pl.BlockSpec((1, D), const),
            pl.BlockSpec((D, SEG), lambda i, n: (0, n)),
            pl.BlockSpec((tm, RET_DIM), pos),
            pl.BlockSpec((tm, RET_DIM), pos),
        ],
        out_specs=pl.BlockSpec((tm, SEG), lambda i, n: (i, n)),
        out_shape=jax.ShapeDtypeStruct((T, IN_COLS), BF16),
        scratch_shapes=[pltpu.VMEM((tm, D), BF16)],
        compiler_params=_params(("parallel", "arbitrary"), vmem),
        name="proj",
    )(x, gain, w_in, cr, sr)


def _attn_kernel(qraw_ref, kraw_ref, vraw_ref, qg_ref, kg_ref, cos_ref, sin_ref, o_ref,
                 q_ref, k_ref, v_ref, po_ref, pl_ref, bias_ref, *by4_refs, seq):
    scale = ATTN_HEAD_DIM ** -0.5
    exp2_scale = scale * LOG2_E
    blocks_per_super = ATTN_SUPER // ATTN_QBLK

    rel = (lax.broadcasted_iota(jnp.int32, (ATTN_QBLK, ATTN_KWIN), 0)
           - lax.broadcasted_iota(jnp.int32, (ATTN_QBLK, ATTN_KWIN), 1))
    for case in range(3):
        bias_ref[case] = jnp.where(jnp.abs(rel + case * ATTN_HALF) <= ATTN_HALF, 0.0, NEG_INF)

    ones = jnp.ones((ATTN_HEAD_DIM, ATTN_HEAD_DIM), BF16)

    def row_sums(t):
        hi = t.astype(BF16)
        lo = (t - hi.astype(F32)).astype(BF16)
        return jnp.dot(hi, ones, preferred_element_type=F32) + jnp.dot(lo, ones, preferred_element_type=F32)

    def qk_norm_rope(i, c):
        rows = pl.ds(pl.multiple_of(i * ATTN_KWIN, ATTN_KWIN), ATTN_KWIN)
        cos, sin = cos_ref[rows, :], sin_ref[rows, :]
        v_ref[rows, :] = vraw_ref[rows, :].astype(F32)
        for raw_ref, gain_ref, dst_ref in ((qraw_ref, qg_ref, q_ref), (kraw_ref, kg_ref, k_ref)):
            y = raw_ref[rows, :].astype(F32)
            mean_sq = row_sums(y * y) * (1.0 / ATTN_HEAD_DIM)
            yn = y * lax.rsqrt(mean_sq + NORM_EPS) * gain_ref[...]
            dst_ref[rows, :] = yn * cos + pltpu.roll(yn, ATTN_HEAD_DIM // 2, 1) * sin
        return c

    by4_ref = by4_refs[0] if by4_refs else None
    n_chunks = seq // ATTN_KWIN

    def regroup(chunk):
        c0 = pl.multiple_of(chunk * ATTN_KWIN, ATTN_KWIN)
        n = ATTN_KWIN // BY4
        dst = pl.ds(pl.multiple_of(chunk * n, n), n)
        for t, src_ref in enumerate((q_ref, k_ref, v_ref)):
            for r in range(BY4):
                by4_ref[t, r, dst, :] = src_ref[pl.ds(c0 + r, n, stride=BY4), :]

    if by4_ref is None:
        lax.fori_loop(0, n_chunks, qk_norm_rope, 0, unroll=4)
    else:
        def regroup_then_norm(t, c):
            regroup(t - 1)
            return qk_norm_rope(t, c)

        qk_norm_rope(jnp.int32(0), 0)
        lax.fori_loop(1, n_chunks, regroup_then_norm, 0, unroll=4)
        regroup(jnp.int32(n_chunks - 1))

    def super_block(sb, carry):
        base = pl.multiple_of(sb * ATTN_SUPER, ATTN_SUPER)

        def blocks(u, c):
            for p, d in enumerate(DILATIONS):
                sub_len = seq // d
                per_res = blocks_per_super // d
                r = u % d
                jl = u // d
                jb = sb * per_res + jl
                q0 = jb * ATTN_QBLK
                k0 = jnp.clip(q0 - ATTN_HALF, 0, sub_len - ATTN_KWIN)
                if d == 1:
                    qs, ks = pl.ds(q0, ATTN_QBLK), pl.ds(k0, ATTN_KWIN)
                    os = pl.ds(jl * ATTN_QBLK, ATTN_QBLK)
                else:
                    qs = pl.ds(r + d * q0, ATTN_QBLK, stride=d)
                    ks = pl.ds(r + d * k0, ATTN_KWIN, stride=d)
                    os = pl.ds(r + d * jl * ATTN_QBLK, ATTN_QBLK, stride=d)
                if by4_ref is not None and d % BY4 == 0:
                    sub, lane_r, off = d // BY4, r % BY4, r // BY4
                    if sub == 1:
                        qs4, ks4 = pl.ds(q0, ATTN_QBLK), pl.ds(k0, ATTN_KWIN)
                    else:
                        qs4 = pl.ds(off + sub * q0, ATTN_QBLK, stride=sub)
                        ks4 = pl.ds(off + sub * k0, ATTN_KWIN, stride=sub)
                    q = by4_ref[0, lane_r, qs4, :].astype(BF16)
                    k = by4_ref[1, lane_r, ks4, :].astype(BF16)
                    v = by4_ref[2, lane_r, ks4, :].astype(BF16)
                else:
                    q = q_ref[qs, :].astype(BF16)
                    k = k_ref[ks, :].astype(BF16)
                    v = v_ref[ks, :].astype(BF16)
                s = lax.dot_general(q, k, (((1,), (1,)), ((), ())), preferred_element_type=F32)
                s = s + bias_ref[(q0 - k0) // ATTN_HALF]
                m = jnp.max(s, axis=-1, keepdims=True)
                e = jnp.exp2((s - m) * exp2_scale)
                den = jnp.sum(e, axis=-1, keepdims=True)
                out = jnp.dot(e.astype(BF16), v, preferred_element_type=F32) / den
                po_ref[p, os, :] = out
                pl_ref[p, os, :] = jnp.broadcast_to(m * exp2_scale + jnp.log2(den), (ATTN_QBLK, ATTN_HEAD_DIM))
            return c

        lax.fori_loop(0, blocks_per_super, blocks, 0, unroll=ATTN_UNROLL)

        def merge(i, c):
            rows = pl.ds(pl.multiple_of(i * ATTN_QBLK, ATTN_QBLK), ATTN_QBLK)
            lse = [pl_ref[p, rows, :] for p in range(len(DILATIONS))]
            top = functools.reduce(jnp.maximum, lse)
            w = [jnp.exp2(l - top) for l in lse]
            tot = functools.reduce(jnp.add, w)
            acc = functools.reduce(jnp.add, [w[p] * po_ref[p, rows, :] for p in range(len(DILATIONS))])
            o_ref[pl.ds(base + i * ATTN_QBLK, ATTN_QBLK), :] = (acc / tot).astype(o_ref.dtype)
            return c

        lax.fori_loop(0, blocks_per_super, merge, 0)
        return carry

    lax.fori_loop(0, seq // ATTN_SUPER, super_block, 0)


def _attn(p3, q_gain, k_gain, tables):
    B, S, _ = p3.shape
    assert S % ATTN_SUPER == 0 and S // DILATIONS[-1] >= ATTN_KWIN
    blk = (None, S, ATTN_HEAD_DIM)
    const = lambda b, h: (0, 0)
    once = dict(pipeline_mode=pl.Buffered(1))
    vmem = 2 * 3 * S * 128 * 2 + 2 * S * 128 * 2 + 2 * S * 128 * 4 + 3 * S * 128 * 4 \
        + 2 * len(DILATIONS) * ATTN_SUPER * 128 * 4 + 4 * MIB
    by4_bytes = 3 * S * ATTN_HEAD_DIM * 4
    by4 = [pltpu.VMEM((3, BY4, S // BY4, ATTN_HEAD_DIM), F32)] if vmem + by4_bytes <= V7X_SCOPED_VMEM_BYTES else []
    vmem += by4_bytes * len(by4)
    return pl.pallas_call(
        functools.partial(_attn_kernel, seq=S),
        grid=(B, ATTN_HEADS),
        in_specs=[
            pl.BlockSpec(blk, lambda b, h: (b, 0, h)),
            pl.BlockSpec(blk, lambda b, h: (b, 0, ATTN_HEADS + h)),
            pl.BlockSpec(blk, lambda b, h: (b, 0, 2 * ATTN_HEADS + h)),
            pl.BlockSpec((1, ATTN_HEAD_DIM), const),
            pl.BlockSpec((1, ATTN_HEAD_DIM), const),
            pl.BlockSpec((S, ATTN_HEAD_DIM), const, **once),
            pl.BlockSpec((S, ATTN_HEAD_DIM), const, **once),
        ],
        out_specs=pl.BlockSpec(blk, lambda b, h: (b, 0, h)),
        out_shape=jax.ShapeDtypeStruct((B, S, ATTN_WIDTH), BF16),
        scratch_shapes=[pltpu.VMEM((S, ATTN_HEAD_DIM), F32),
                        pltpu.VMEM((S, ATTN_HEAD_DIM), F32),
                        pltpu.VMEM((S, ATTN_HEAD_DIM), F32),
                        pltpu.VMEM((len(DILATIONS), ATTN_SUPER, ATTN_HEAD_DIM), F32),
                        pltpu.VMEM((len(DILATIONS), ATTN_SUPER, ATTN_HEAD_DIM), F32),
                        pltpu.VMEM((3, ATTN_QBLK, ATTN_KWIN), F32)] + by4,
        compiler_params=_params(("parallel", "parallel"), vmem, claim_all=True),
        name="attn",
    )(p3, p3, p3, q_gain, k_gain, *tables)


def _decay_tables(lg, backward):
    C = RET_CHUNK
    i = lax.broadcasted_iota(jnp.int32, (C, C), 0)
    j = lax.broadcasted_iota(jnp.int32, (C, C), 1)
    diff = (j - i) if backward else (i - j)
    mask = jnp.where(diff >= 0, jnp.exp(lg * jnp.maximum(diff, 0).astype(F32)), 0.0)
    idx = lax.broadcasted_iota(jnp.int32, (C, 1), 0).astype(F32)
    if backward:
        k_pow, q_pow = idx, C - idx
    else:
        k_pow, q_pow = C - 1 - idx, idx + 1
    chunk = jnp.exp(jnp.full((1, RET_DIM), C, F32) * lg)
    return mask, jnp.exp(lg * k_pow), jnp.exp(lg * q_pow), chunk


def _ret_cross(q, k, vb, state, k_scale, q_scale, chunk_decay):
    cross = jnp.dot((q * q_scale).astype(BF16), state.astype(BF16), preferred_element_type=F32)
    kv = lax.dot_general((k * k_scale).astype(BF16), vb, (((0,), (0,)), ((), ())), preferred_element_type=F32)
    return cross, chunk_decay * state + kv


def _ret_bwd_rows(lg, q_ref, k_ref, v_ref, o_ref, state_ref, row0):
    _, k_scale, q_scale, chunk_decay = _decay_tables(lg, backward=True)
    state = state_ref[...]
    for c in reversed(range(RET_ROWS // RET_CHUNK)):
        rows = slice(row0 + c * RET_CHUNK, row0 + (c + 1) * RET_CHUNK)
        cross, state = _ret_cross(q_ref[rows, :].astype(F32), k_ref[rows, :].astype(F32),
                                  v_ref[rows, :].astype(BF16), state, k_scale, q_scale, chunk_decay)
        o_ref[rows, :] = cross
    state_ref[...] = state


def _ret_fwd_rows(lg_fwd, lg_bwd, q_ref, k_ref, v_ref, gate_ref, bwd_ref, gain_ref, o_ref, state_ref, row0):
    mask_f, k_scale, q_scale, chunk_decay = _decay_tables(lg_fwd, backward=False)
    mask = mask_f + _decay_tables(lg_bwd, backward=True)[0]
    state = state_ref[...]
    for c in range(RET_ROWS // RET_CHUNK):
        rows = slice(row0 + c * RET_CHUNK, row0 + (c + 1) * RET_CHUNK)
        q, k, vb = q_ref[rows, :].astype(F32), k_ref[rows, :].astype(F32), v_ref[rows, :].astype(BF16)
        s = lax.dot_general(q.astype(BF16), k.astype(BF16), (((1,), (1,)), ((), ())), preferred_element_type=F32)
        inner = jnp.dot((s * mask).astype(BF16), vb, preferred_element_type=F32)
        cross, state = _ret_cross(q, k, vb, state, k_scale, q_scale, chunk_decay)
        both = _rms_norm_rows(inner + cross + bwd_ref[rows, :], gain_ref[...])
        o_ref[rows, :] = (jax.nn.silu(gate_ref[rows, :].astype(F32)) * both).astype(o_ref.dtype)
    state_ref[...] = state


def _ret_bwd_kernel(ld_ref, q_ref, k_ref, v_ref, o_ref, state_ref):
    h, t = pl.program_id(1), pl.program_id(2)

    @pl.when(t == 0)
    def _():
        state_ref[...] = jnp.zeros_like(state_ref)

    _ret_bwd_rows(ld_ref[h], q_ref, k_ref, v_ref, o_ref, state_ref, 0)


def _ret_fwd_kernel(ldf_ref, ldb_ref, q_ref, k_ref, v_ref, gate_ref, bwd_ref, gain_ref, o_ref, state_ref):
    h, t = pl.program_id(1), pl.program_id(2)

    @pl.when(t == 0)
    def _():
        state_ref[...] = jnp.zeros_like(state_ref)

    _ret_fwd_rows(ldf_ref[h], ldb_ref[h], q_ref, k_ref, v_ref, gate_ref, bwd_ref, gain_ref, o_ref, state_ref, 0)


def _ret_fused_kernel(ldf_ref, ldb_ref, q_ref, k_ref, v_ref, gate_ref, gain_ref, o_ref, bwd_ref, state_ref, *, seq):
    h = pl.program_id(1)
    blocks = range(0, seq, RET_ROWS)
    state_ref[...] = jnp.zeros_like(state_ref)
    for row0 in reversed(blocks):
        _ret_bwd_rows(ldb_ref[h], q_ref, k_ref, v_ref, bwd_ref, state_ref, row0)
    state_ref[...] = jnp.zeros_like(state_ref)
    for row0 in blocks:
        _ret_fwd_rows(ldf_ref[h], ldb_ref[h], q_ref, k_ref, v_ref, gate_ref, bwd_ref, gain_ref, o_ref, state_ref, row0)


def _ret_specs(n_t, reverse):
    tmap = (lambda t: n_t - 1 - t) if reverse else (lambda t: t)
    blk = (None, RET_ROWS, RET_DIM)
    seg0 = 3 * ATTN_WIDTH // RET_DIM

    def col(seg):
        return pl.BlockSpec(blk, lambda b, h, t: (b, tmap(t), seg0 + seg * RET_HEADS + h))

    return blk, tmap, col


def _ret_bwd(p3, log_decay):
    B, S, _ = p3.shape
    n_t = S // RET_ROWS
    blk, tmap, col = _ret_specs(n_t, reverse=True)
    vmem = 2 * 4 * RET_ROWS * RET_DIM * 4 + RET_DIM * RET_DIM * 4 + 8 * MIB
    return pl.pallas_call(
        _ret_bwd_kernel,
        grid=(B, RET_HEADS, n_t),
        in_specs=[pl.BlockSpec(memory_space=pltpu.SMEM), col(0), col(1), col(2)],
        out_specs=pl.BlockSpec(blk, lambda b, h, t: (b, tmap(t), h)),
        out_shape=jax.ShapeDtypeStruct((B, S, RET_WIDTH), F32),
        scratch_shapes=[pltpu.VMEM((RET_DIM, RET_DIM), F32)],
        compiler_params=_params(("parallel", "parallel", "arbitrary"), vmem, claim_all=True),
        name="ret_bwd",
    )(log_decay, p3, p3, p3)


def _ret_fwd(p3, bwd, log_decay_fwd, log_decay_bwd, out_gain):
    B, S, _ = p3.shape
    n_t = S // RET_ROWS
    blk, tmap, col = _ret_specs(n_t, reverse=False)
    vmem = 2 * 6 * RET_ROWS * RET_DIM * 4 + RET_DIM * RET_DIM * 4 + 8 * MIB
    smem = pl.BlockSpec(memory_space=pltpu.SMEM)
    return pl.pallas_call(
        _ret_fwd_kernel,
        grid=(B, RET_HEADS, n_t),
        in_specs=[smem, smem, col(0), col(1), col(2), col(3),
                  pl.BlockSpec(blk, lambda b, h, t: (b, t, h)),
                  pl.BlockSpec((None, 1, RET_DIM), lambda b, h, t: (h, 0, 0))],
        out_specs=pl.BlockSpec(blk, lambda b, h, t: (b, t, h)),
        out_shape=jax.ShapeDtypeStruct((B, S, RET_WIDTH), BF16),
        scratch_shapes=[pltpu.VMEM((RET_DIM, RET_DIM), F32)],
        compiler_params=_params(("parallel", "parallel", "arbitrary"), vmem, claim_all=True),
        name="ret_fwd",
    )(log_decay_fwd, log_decay_bwd, p3, p3, p3, p3, bwd, out_gain)


def _retention(p3, log_decay_fwd, log_decay_bwd, out_gain):
    B, S, _ = p3.shape
    seq_bytes = S * RET_DIM * 4
    fused_vmem = 2 * 4 * seq_bytes // 2 + seq_bytes + 2 * seq_bytes // 2 + RET_DIM * RET_DIM * 4 + 6 * MIB
    if fused_vmem > V7X_SCOPED_VMEM_BYTES:
        bwd = _ret_bwd(p3, log_decay_bwd)
        return _ret_fwd(p3, bwd, log_decay_fwd, log_decay_bwd, out_gain)
    blk = (None, S, RET_DIM)
    seg0 = 3 * ATTN_WIDTH // RET_DIM
    col = lambda seg: pl.BlockSpec(blk, lambda b, h: (b, 0, seg0 + seg * RET_HEADS + h))
    smem = pl.BlockSpec(memory_space=pltpu.SMEM)
    return pl.pallas_call(
        functools.partial(_ret_fused_kernel, seq=S),
        grid=(B, RET_HEADS),
        in_specs=[smem, smem, col(0), col(1), col(2), col(3),
                  pl.BlockSpec((None, 1, RET_DIM), lambda b, h: (h, 0, 0))],
        out_specs=pl.BlockSpec(blk, lambda b, h: (b, 0, h)),
        out_shape=jax.ShapeDtypeStruct((B, S, RET_WIDTH), BF16),
        scratch_shapes=[pltpu.VMEM((S, RET_DIM), F32), pltpu.VMEM((RET_DIM, RET_DIM), F32)],
        compiler_params=_params(("parallel", "parallel"), fused_vmem, claim_all=True),
        name="ret_fused",
    )(log_decay_fwd, log_decay_bwd, p3, p3, p3, p3, out_gain)


def _mix_out_kernel(x_ref, a_ref, r_ref, w_ref, o_ref):
    acc = jnp.dot(a_ref[...], w_ref[:ATTN_WIDTH, :], preferred_element_type=F32)
    acc += jnp.dot(r_ref[...], w_ref[ATTN_WIDTH:, :], preferred_element_type=F32)
    o_ref[...] = x_ref[...] + acc


def _mix_out(x, a, r, w_out):
    T, D = x.shape
    tm = MIX_ROWS
    vmem = 2 * (2 * tm * D * 4 + 2 * tm * SEG * 2) + D * D * 2 + tm * D * 4 + 4 * MIB
    return pl.pallas_call(
        _mix_out_kernel,
        grid=(T // tm,),
        in_specs=[
            pl.BlockSpec((tm, D), lambda i: (i, 0)),
            pl.BlockSpec((tm, ATTN_WIDTH), lambda i: (i, 0)),
            pl.BlockSpec((tm, RET_WIDTH), lambda i: (i, 0)),
            pl.BlockSpec((D, D), lambda i: (0, 0), pipeline_mode=pl.Buffered(1)),
        ],
        out_specs=pl.BlockSpec((tm, D), lambda i: (i, 0)),
        out_shape=jax.ShapeDtypeStruct((T, D), F32),
        compiler_params=_params(("parallel",), vmem, claim_all=True),
        name="mix_out",
    )(x, a, r, w_out)


_MIXER_WEIGHTS = ("w_in", "w_out")
_FFN1_WEIGHTS = ("ffn1_wg", "ffn1_wu", "ffn1_wd")
_FFN2_WEIGHTS = ("ffn2_wg", "ffn2_wu", "ffn2_wd")
_WEIGHT_PADS = {"w_in": (D_MODEL, IN_COLS), "w_out": (D_MODEL, D_MODEL),
                "ffn1_wg": (D_MODEL, D_FF_PAD), "ffn1_wu": (D_MODEL, D_FF_PAD), "ffn1_wd": (D_FF_PAD, D_MODEL),
                "ffn2_wg": (D_MODEL, D_FF_PAD), "ffn2_wu": (D_MODEL, D_FF_PAD), "ffn2_wd": (D_FF_PAD, D_MODEL)}


def _trunk(x3, layers, f32_weights, bf16_weights):
    B, S, D = x3.shape
    T = B * S
    x = x3.reshape(T, D)
    attn_tables, ret_tables = _rope_tables(S, ATTN_HEAD_DIM), _rope_tables(S, RET_DIM)

    def ffn_carrying(x, gain, own, layer, carried, carried_layer):
        names = [n for n in carried if carried_layer < len(layers) and (n, carried_layer) not in bf16_weights]
        casts = [_CarriedCast(f32_weights[n], carried_layer, *_WEIGHT_PADS[n]) for n in names]
        y, cast = _ffn(x, gain, *[bf16_weights[(n, layer)] for n in own], casts=casts)
        bf16_weights.update({(n, carried_layer): w for n, w in zip(names, cast)})
        return y

    for l, lw in enumerate(layers):
        x = ffn_carrying(x, lw["ffn1_norm"], _FFN1_WEIGHTS, l, _MIXER_WEIGHTS + _FFN2_WEIGHTS, l)
        proj = _proj(x, lw["mix_norm"], bf16_weights[("w_in", l)], ret_tables, S)
        p3 = proj.reshape(B, S, IN_COLS)
        a = _attn(p3, lw["q_gain"], lw["k_gain"], attn_tables)
        r = _retention(p3, lw["ld_fwd"], lw["ld_bwd"], lw["ret_gain"])
        x = _mix_out(x, a.reshape(T, ATTN_WIDTH), r.reshape(T, RET_WIDTH), bf16_weights[("w_out", l)])
        x = ffn_carrying(x, lw["ffn2_norm"], _FFN2_WEIGHTS, l, _FFN1_WEIGHTS, l + 1)
    return x.reshape(B, S, D)


def kernel(x_prompt, x_sample, ffn1_norm, ffn1_w_gate, ffn1_w_up, ffn1_w_down, mix_norm, w_in, attn_q_norm, attn_k_norm, ret_log_decay_fwd, ret_log_decay_bwd, ret_out_norm, w_out, ffn2_norm, ffn2_w_gate, ffn2_w_up, ffn2_w_down):
    depth = w_in.shape[0]
    layers = []
    for l in range(depth):
        layers.append({
            "ffn1_norm": ffn1_norm[l][None, :].astype(F32),
            "mix_norm": mix_norm[l][None, :].astype(F32),
            "q_gain": attn_q_norm[l][None, :].astype(F32),
            "k_gain": attn_k_norm[l][None, :].astype(F32),
            "ld_fwd": ret_log_decay_fwd[l].astype(F32),
            "ld_bwd": ret_log_decay_bwd[l].astype(F32),
            "ret_gain": ret_out_norm[l][:, None, :].astype(F32),
            "ffn2_norm": ffn2_norm[l][None, :].astype(F32),
        })
    f32_weights = {"w_in": w_in, "w_out": w_out,
                   "ffn1_wg": ffn1_w_gate, "ffn1_wu": ffn1_w_up, "ffn1_wd": ffn1_w_down,
                   "ffn2_wg": ffn2_w_gate, "ffn2_wu": ffn2_w_up, "ffn2_wd": ffn2_w_down}
    bf16_weights = {(n, 0): _cast_layer(f32_weights[n], 0, *_WEIGHT_PADS[n]) for n in _FFN1_WEIGHTS}
    y_sample = _trunk(x_sample, layers, f32_weights, bf16_weights)
    y_prompt = _trunk(x_prompt, layers, f32_weights, bf16_weights)
    return (y_prompt, y_sample)
```

```python
import functools
from typing import NamedTuple

import jax
import jax.numpy as jnp
from jax import lax
from jax.experimental import pallas as pl
from jax.experimental.pallas import tpu as pltpu

F32 = jnp.float32
BF16 = jnp.bfloat16

D_MODEL = 2048
ATTN_HEAD_DIM = 128
ATTN_HEADS = 8
ATTN_WIDTH = ATTN_HEADS * ATTN_HEAD_DIM
DILATIONS = (1, 4, 16)
ATTN_HALF = 64
RET_HEADS = 4
RET_DIM = 256
RET_WIDTH = RET_HEADS * RET_DIM
RET_CHUNK = 128
D_FF = 5504
IN_COLS = 3 * ATTN_WIDTH + 4 * RET_WIDTH
ROPE_THETA = 10000.0
NORM_EPS = 1e-6
NEG_INF = -1e30
LOG2_E = 1.4426950408889634

V7X_LANES = 128
V7X_BF16_SUBLANES = 16
V7X_SCOPED_VMEM_BYTES = 56 * 1024 * 1024
FF_TILE = 512
D_FF_PAD = -(-D_FF // FF_TILE) * FF_TILE
MIX_ROWS = 512
FFN_ROWS = 1024
PROJ_ROWS = 1024
CAST_TILE_ELEMS = 1024 * 1024
SEG = 1024
PROJ_CHUNK = RET_DIM
ATTN_QBLK = 128
ATTN_KWIN = 256
ATTN_SUPER = ATTN_QBLK * DILATIONS[-1]
BY4 = DILATIONS[1]
ATTN_UNROLL = 16
RET_ROWS = 1024
MIB = 1024 * 1024


def _params(semantics, vmem_bytes, claim_all=False):
    if claim_all:
        assert vmem_bytes <= V7X_SCOPED_VMEM_BYTES
        vmem_bytes = V7X_SCOPED_VMEM_BYTES
    return pltpu.CompilerParams(dimension_semantics=semantics, vmem_limit_bytes=int(vmem_bytes))


def _rms_norm_rows(x, gain):
    return x * lax.rsqrt(jnp.mean(x * x, axis=-1, keepdims=True) + NORM_EPS) * gain


def _cast_tile(w_ref, o_ref, row_tile, col_tile, rows, cols):
    tr, tc = o_ref.shape
    w = w_ref[...]
    if rows % tr or cols % tc:
        r = lax.broadcasted_iota(jnp.int32, (tr, tc), 0) + row_tile * tr
        c = lax.broadcasted_iota(jnp.int32, (tr, tc), 1) + col_tile * tc
        w = jnp.where((r < rows) & (c < cols), w, 0.0)
    o_ref[...] = w.astype(o_ref.dtype)


def _cast_kernel(w_ref, o_ref, *, rows, cols):
    _cast_tile(w_ref, o_ref, pl.program_id(0), pl.program_id(1), rows, cols)


class _CarriedCast(NamedTuple):
    w: jax.Array
    layer: int
    rows_pad: int
    cols_pad: int

    def tiling(self, n_row_blocks, n_col_steps):
        tr = self.rows_pad // n_row_blocks
        n_col_tiles = max(n for n in range(1, n_col_steps + 1)
                          if self.cols_pad % n == 0 and (self.cols_pad // n) % V7X_LANES == 0)
        assert self.rows_pad % n_row_blocks == 0 and tr % V7X_BF16_SUBLANES == 0
        return tr, self.cols_pad // n_col_tiles, n_col_tiles

    def specs(self, n_row_blocks, n_col_steps):
        tr, tc, n_col_tiles = self.tiling(n_row_blocks, n_col_steps)
        layer = self.layer
        col = lambda f: jnp.minimum(f, n_col_tiles - 1)
        return (pl.BlockSpec((None, tr, tc), lambda i, f: (layer, i, col(f))),
                pl.BlockSpec((tr, tc), lambda i, f: (i, col(f))),
                jax.ShapeDtypeStruct((self.rows_pad, self.cols_pad), BF16))


def _run_carried_casts(extents, in_refs, out_refs, i, f):
    for (rows, cols, n_col_tiles), w_ref, o_ref in zip(extents, in_refs, out_refs):
        _cast_tile(w_ref, o_ref, i, jnp.minimum(f, n_col_tiles - 1), rows, cols)


def _cast_layer(w, layer, rows_pad, cols_pad):
    _, rows, cols = w.shape
    tc = next(c for c in (2048, 1024, 512) if cols_pad % c == 0)
    tr = CAST_TILE_ELEMS // tc
    assert rows_pad % tr == 0
    return pl.pallas_call(
        functools.partial(_cast_kernel, rows=rows, cols=cols),
        grid=(rows_pad // tr, cols_pad // tc),
        in_specs=[pl.BlockSpec((None, tr, tc), lambda i, j: (layer, i, j))],
        out_specs=pl.BlockSpec((tr, tc), lambda i, j: (i, j)),
        out_shape=jax.ShapeDtypeStruct((rows_pad, cols_pad), BF16),
        compiler_params=_params(("parallel", "parallel"), 2 * tr * tc * (4 + 2) + 4 * MIB, claim_all=True),
        name="cast",
    )(w)


def _ffn_kernel(x_ref, g_ref, wg_ref, wu_ref, wd_ref, *refs, cast_extents):
    n_casts = len(cast_extents)
    cast_in, o_ref, cast_out, xn_ref = refs[:n_casts], refs[n_casts], refs[n_casts + 1:-1], refs[-1]
    i, f = pl.program_id(0), pl.program_id(1)
    last = pl.num_programs(1) - 1
    tm = x_ref.shape[0]

    def step(rows, first, final):
        if first:
            xn_ref[rows, :] = _rms_norm_rows(x_ref[rows, :], g_ref[...]).astype(BF16)
        xn = xn_ref[rows, :]
        gate = jnp.dot(xn, wg_ref[...], preferred_element_type=F32)
        up = jnp.dot(xn, wu_ref[...], preferred_element_type=F32)
        h = (jax.nn.silu(gate) * up).astype(BF16)
        acc = jnp.dot(h, wd_ref[...], preferred_element_type=F32)
        if not first:
            acc = o_ref[rows, :] + acc
        o_ref[rows, :] = x_ref[rows, :] + 0.5 * acc if final else acc

    halves = (slice(0, tm // 2), slice(tm // 2, tm))

    @pl.when(f == 0)
    def _():
        _run_carried_casts(cast_extents, cast_in, cast_out, i, f)
        for rows in halves:
            step(rows, True, False)

    @pl.when((f > 0) & (f < last))
    def _():
        _run_carried_casts(cast_extents, cast_in, cast_out, i, f)
        step(slice(0, tm), False, False)

    @pl.when(f == last)
    def _():
        _run_carried_casts(cast_extents, cast_in, cast_out, i, f)
        for rows in halves:
            step(rows, False, True)


def _ffn(x, gain, wg, wu, wd, casts=()):
    T, D = x.shape
    tm, tf = FFN_ROWS, FF_TILE
    grid = (T // tm, D_FF_PAD // tf)
    cast_specs = [c.specs(*grid) for c in casts]
    cast_extents = tuple(c.w.shape[1:] + (c.tiling(*grid)[2],) for c in casts)
    cast_vmem = sum(2 * tr * tc * (4 + 2) for tr, tc, _ in (c.tiling(*grid) for c in casts))
    vmem = 2 * (2 * tm * D * 4) + tm * D * 2 + 2 * 3 * D * tf * 2 + 3 * tm * tf * 4 + cast_vmem + 2 * MIB
    y, *cast_out = pl.pallas_call(
        functools.partial(_ffn_kernel, cast_extents=cast_extents),
        grid=grid,
        in_specs=[
            pl.BlockSpec((tm, D), lambda i, f: (i, 0)),
            pl.BlockSpec((1, D), lambda i, f: (0, 0)),
            pl.BlockSpec((D, tf), lambda i, f: (0, f)),
            pl.BlockSpec((D, tf), lambda i, f: (0, f)),
            pl.BlockSpec((tf, D), lambda i, f: (f, 0)),
        ] + [s[0] for s in cast_specs],
        out_specs=[pl.BlockSpec((tm, D), lambda i, f: (i, 0))] + [s[1] for s in cast_specs],
        out_shape=[jax.ShapeDtypeStruct((T, D), F32)] + [s[2] for s in cast_specs],
        scratch_shapes=[pltpu.VMEM((tm, D), BF16)],
        compiler_params=_params(("parallel", "arbitrary"), vmem),
        name="ffn",
    )(x, gain, wg, wu, wd, *[c.w for c in casts])
    return y, cast_out


def _proj_kernel(x_ref, g_ref, w_ref, cr_ref, sr_ref, o_ref, xn_ref):
    n = pl.program_id(1)

    tm = x_ref.shape[0]

    def by_chunks(epilogue, rows=slice(None)):
        xn = xn_ref[rows, :]
        for j in range(SEG // PROJ_CHUNK):
            c0 = j * PROJ_CHUNK
            y = jnp.dot(xn, w_ref[:, c0:c0 + PROJ_CHUNK], preferred_element_type=F32)
            epilogue(y, c0, rows)

    def ret_qk(scale):
        def epilogue(y, c0, rows):
            cos, sin = cr_ref[rows, :], sr_ref[rows, :]
            half = RET_DIM // 2
            lo, hi = y[:, :half], y[:, half:]
            o_ref[rows, c0:c0 + half] = ((lo * cos[:, :half] + hi * sin[:, :half]) * scale).astype(o_ref.dtype)
            o_ref[rows, c0 + half:c0 + RET_DIM] = ((hi * cos[:, half:] + lo * sin[:, half:]) * scale).astype(o_ref.dtype)
        by_chunks(epilogue)

    def plain(rows=slice(None)):
        o_ref[rows, :] = jnp.dot(xn_ref[rows, :], w_ref[...], preferred_element_type=F32).astype(o_ref.dtype)

    @pl.when(n == 0)
    def _():
        for rows in (slice(0, tm // 2), slice(tm // 2, tm)):
            xn_ref[rows, :] = _rms_norm_rows(x_ref[rows, :], g_ref[...]).astype(BF16)
            plain(rows)

    @pl.when(n == 3)
    def _():
        ret_qk(1.0)

    @pl.when(n == 4)
    def _():
        ret_qk(RET_DIM ** -0.5)

    @pl.when(((n > 0) & (n < 3)) | (n >= 5))
    def _():
        plain()


def _rope_tables(seq, head_dim):
    half = head_dim // 2
    inv_freq = ROPE_THETA ** (-jnp.arange(half, dtype=F32) / half)
    ang = jnp.arange(seq, dtype=F32)[:, None] * inv_freq[None, :]
    cos, sin = jnp.cos(ang), jnp.sin(ang)
    return jnp.concatenate([cos, cos], axis=-1), jnp.concatenate([-sin, sin], axis=-1)


def _proj(x, gain, w_in, ret_tables, seq):
    T, D = x.shape
    tm = PROJ_ROWS
    nseq = seq // tm
    cr, sr = ret_tables
    pos = lambda i, n: (i % nseq, 0)
    const = lambda i, n: (0, 0)
    vmem = 2 * tm * D * 4 + tm * D * 2 + 2 * D * SEG * 2 + 2 * tm * SEG * 2 + 2 * 2 * tm * RET_DIM * 4 \
        + tm * SEG * 4 + 4 * MIB
    return pl.pallas_call(
        _proj_kernel,
        grid=(T // tm, IN_COLS // SEG),
        in_specs=[
            pl.BlockSpec((tm, D), lambda i, n: (i, 0)),
            pl.BlockSpec((1, D), const),
            pl.BlockSpec((D, SEG), lambda i, n: (0, n)),
            pl.BlockSpec((tm, RET_DIM), pos),
            pl.BlockSpec((tm, RET_DIM), pos),
        ],
        out_specs=pl.BlockSpec((tm, SEG), lambda i, n: (i, n)),
        out_shape=jax.ShapeDtypeStruct((T, IN_COLS), BF16),
        scratch_shapes=[pltpu.VMEM((tm, D), BF16)],
        compiler_params=_params(("parallel", "arbitrary"), vmem),
        name="proj",
    )(x, gain, w_in, cr, sr)


def _attn_kernel(qraw_ref, kraw_ref, vraw_ref, qg_ref, kg_ref, cos_ref, sin_ref, o_ref,
                 q_ref, k_ref, v_ref, po_ref, pl_ref, bias_ref, *by4_refs, seq):
    scale = ATTN_HEAD_DIM ** -0.5
    exp2_scale = scale * LOG2_E
    blocks_per_super = ATTN_SUPER // ATTN_QBLK

    rel = (lax.broadcasted_iota(jnp.int32, (ATTN_QBLK, ATTN_KWIN), 0)
           - lax.broadcasted_iota(jnp.int32, (ATTN_QBLK, ATTN_KWIN), 1))
    for case in range(3):
        bias_ref[case] = jnp.where(jnp.abs(rel + case * ATTN_HALF) <= ATTN_HALF, 0.0, NEG_INF)

    ones = jnp.ones((ATTN_HEAD_DIM, ATTN_HEAD_DIM), BF16)
    ones_kv = jnp.ones((ATTN_KWIN, ATTN_HEAD_DIM), BF16)

    def row_sums(t):
        hi = t.astype(BF16)
        lo = (t - hi.astype(F32)).astype(BF16)
        return jnp.dot(hi, ones, preferred_element_type=F32) + jnp.dot(lo, ones, preferred_element_type=F32)

    def qk_norm_rope(i, c):
        rows = pl.ds(pl.multiple_of(i * ATTN_KWIN, ATTN_KWIN), ATTN_KWIN)
        cos, sin = cos_ref[rows, :], sin_ref[rows, :]
        v_ref[rows, :] = vraw_ref[rows, :].astype(F32)
        for raw_ref, gain_ref, dst_ref in ((qraw_ref, qg_ref, q_ref), (kraw_ref, kg_ref, k_ref)):
            y = raw_ref[rows, :].astype(F32)
            mean_sq = row_sums(y * y) * (1.0 / ATTN_HEAD_DIM)
            yn = y * lax.rsqrt(mean_sq + NORM_EPS) * gain_ref[...]
            dst_ref[rows, :] = yn * cos + pltpu.roll(yn, ATTN_HEAD_DIM // 2, 1) * sin
        return c

    by4_ref = by4_refs[0] if by4_refs else None
    n_chunks = seq // ATTN_KWIN

    def regroup(chunk):
        c0 = pl.multiple_of(chunk * ATTN_KWIN, ATTN_KWIN)
        n = ATTN_KWIN // BY4
        dst = pl.ds(pl.multiple_of(chunk * n, n), n)
        for t, src_ref in enumerate((q_ref, k_ref, v_ref)):
            for r in range(BY4):
                by4_ref[t, r, dst, :] = src_ref[pl.ds(c0 + r, n, stride=BY4), :]

    if by4_ref is None:
        lax.fori_loop(0, n_chunks, qk_norm_rope, 0, unroll=4)
    else:
        def regroup_then_norm(t, c):
            regroup(t - 1)
            return qk_norm_rope(t, c)

        qk_norm_rope(jnp.int32(0), 0)
        lax.fori_loop(1, n_chunks, regroup_then_norm, 0, unroll=4)
        regroup(jnp.int32(n_chunks - 1))

    def super_block(sb, carry):
        base = pl.multiple_of(sb * ATTN_SUPER, ATTN_SUPER)

        def blocks(u, c):
            for p, d in enumerate(DILATIONS):
                sub_len = seq // d
                per_res = blocks_per_super // d
                r = u % d
                jl = u // d
                jb = sb * per_res + jl
                q0 = jb * ATTN_QBLK
                k0 = jnp.clip(q0 - ATTN_HALF, 0, sub_len - ATTN_KWIN)
                if d == 1:
                    qs, ks = pl.ds(q0, ATTN_QBLK), pl.ds(k0, ATTN_KWIN)
                    os = pl.ds(jl * ATTN_QBLK, ATTN_QBLK)
                else:
                    qs = pl.ds(r + d * q0, ATTN_QBLK, stride=d)
                    ks = pl.ds(r + d * k0, ATTN_KWIN, stride=d)
                    os = pl.ds(r + d * jl * ATTN_QBLK, ATTN_QBLK, stride=d)
                if by4_ref is not None and d % BY4 == 0:
                    sub, lane_r, off = d // BY4, r % BY4, r // BY4
                    if sub == 1:
                        qs4, ks4 = pl.ds(q0, ATTN_QBLK), pl.ds(k0, ATTN_KWIN)
                    else:
                        qs4 = pl.ds(off + sub * q0, ATTN_QBLK, stride=sub)
                        ks4 = pl.ds(off + sub * k0, ATTN_KWIN, stride=sub)
                    q = by4_ref[0, lane_r, qs4, :].astype(BF16)
                    k = by4_ref[1, lane_r, ks4, :].astype(BF16)
                    v = by4_ref[2, lane_r, ks4, :].astype(BF16)
                else:
                    q = q_ref[qs, :].astype(BF16)
                    k = k_ref[ks, :].astype(BF16)
                    v = v_ref[ks, :].astype(BF16)
                s = lax.dot_general(q, k, (((1,), (1,)), ((), ())), preferred_element_type=F32)
                s = s + bias_ref[(q0 - k0) // ATTN_HALF]
                m = jnp.max(s, axis=-1, keepdims=True)
                e = jnp.exp2((s - m) * exp2_scale)
                both = jnp.dot(e.astype(BF16), jnp.concatenate([v, ones_kv], axis=1), preferred_element_type=F32)
                den = both[:, ATTN_HEAD_DIM:]
                po_ref[p, os, :] = both[:, :ATTN_HEAD_DIM] / den
                pl_ref[p, os, :] = m * exp2_scale + jnp.log2(den)
            return c

        lax.fori_loop(0, blocks_per_super, blocks, 0, unroll=ATTN_UNROLL)

        def merge(i, c):
            rows = pl.ds(pl.multiple_of(i * ATTN_QBLK, ATTN_QBLK), ATTN_QBLK)
            lse = [pl_ref[p, rows, :] for p in range(len(DILATIONS))]
            top = functools.reduce(jnp.maximum, lse)
            w = [jnp.exp2(l - top) for l in lse]
            tot = functools.reduce(jnp.add, w)
            acc = functools.reduce(jnp.add, [w[p] * po_ref[p, rows, :] for p in range(len(DILATIONS))])
            o_ref[pl.ds(base + i * ATTN_QBLK, ATTN_QBLK), :] = (acc / tot).astype(o_ref.dtype)
            return c

        lax.fori_loop(0, blocks_per_super, merge, 0)
        return carry

    lax.fori_loop(0, seq // ATTN_SUPER, super_block, 0)


def _attn(p3, q_gain, k_gain, tables):
    B, S, _ = p3.shape
    assert S % ATTN_SUPER == 0 and S // DILATIONS[-1] >= ATTN_KWIN
    blk = (None, S, ATTN_HEAD_DIM)
    const = lambda b, h: (0, 0)
    once = dict(pipeline_mode=pl.Buffered(1))
    vmem = 2 * 3 * S * 128 * 2 + 2 * S * 128 * 2 + 2 * S * 128 * 4 + 3 * S * 128 * 4 \
        + 2 * len(DILATIONS) * ATTN_SUPER * 128 * 4 + 4 * MIB
    by4_bytes = 3 * S * ATTN_HEAD_DIM * 4
    by4 = [pltpu.VMEM((3, BY4, S // BY4, ATTN_HEAD_DIM), F32)] if vmem + by4_bytes <= V7X_SCOPED_VMEM_BYTES else []
    vmem += by4_bytes * len(by4)
    return pl.pallas_call(
        functools.partial(_attn_kernel, seq=S),
        grid=(B, ATTN_HEADS),
        in_specs=[
            pl.BlockSpec(blk, lambda b, h: (b, 0, h)),
            pl.BlockSpec(blk, lambda b, h: (b, 0, ATTN_HEADS + h)),
            pl.BlockSpec(blk, lambda b, h: (b, 0, 2 * ATTN_HEADS + h)),
            pl.BlockSpec((1, ATTN_HEAD_DIM), const),
            pl.BlockSpec((1, ATTN_HEAD_DIM), const),
            pl.BlockSpec((S, ATTN_HEAD_DIM), const, **once),
            pl.BlockSpec((S, ATTN_HEAD_DIM), const, **once),
        ],
        out_specs=pl.BlockSpec(blk, lambda b, h: (b, 0, h)),
        out_shape=jax.ShapeDtypeStruct((B, S, ATTN_WIDTH), BF16),
        scratch_shapes=[pltpu.VMEM((S, ATTN_HEAD_DIM), F32),
                        pltpu.VMEM((S, ATTN_HEAD_DIM), F32),
                        pltpu.VMEM((S, ATTN_HEAD_DIM), F32),
                        pltpu.VMEM((len(DILATIONS), ATTN_SUPER, ATTN_HEAD_DIM), F32),
                        pltpu.VMEM((len(DILATIONS), ATTN_SUPER, ATTN_HEAD_DIM), F32),
                        pltpu.VMEM((3, ATTN_QBLK, ATTN_KWIN), F32)] + by4,
        compiler_params=_params(("parallel", "parallel"), vmem, claim_all=True),
        name="attn",
    )(p3, p3, p3, q_gain, k_gain, *tables)


def _decay_tables(lg, backward):
    C = RET_CHUNK
    i = lax.broadcasted_iota(jnp.int32, (C, C), 0)
    j = lax.broadcasted_iota(jnp.int32, (C, C), 1)
    diff = (j - i) if backward else (i - j)
    mask = jnp.where(diff >= 0, jnp.exp(lg * jnp.maximum(diff, 0).astype(F32)), 0.0)
    idx = lax.broadcasted_iota(jnp.int32, (C, 1), 0).astype(F32)
    if backward:
        k_pow, q_pow = idx, C - idx
    else:
        k_pow, q_pow = C - 1 - idx, idx + 1
    chunk = jnp.exp(jnp.full((1, RET_DIM), C, F32) * lg)
    return mask, jnp.exp(lg * k_pow), jnp.exp(lg * q_pow), chunk


def _ret_cross(q, k, vb, state, k_scale, q_scale, chunk_decay):
    cross = jnp.dot((q * q_scale).astype(BF16), state.astype(BF16), preferred_element_type=F32)
    kv = lax.dot_general((k * k_scale).astype(BF16), vb, (((0,), (0,)), ((), ())), preferred_element_type=F32)
    return cross, chunk_decay * state + kv


def _ret_bwd_rows(lg, q_ref, k_ref, v_ref, o_ref, state_ref, row0):
    _, k_scale, q_scale, chunk_decay = _decay_tables(lg, backward=True)
    state = state_ref[...]
    for c in reversed(range(RET_ROWS // RET_CHUNK)):
        rows = slice(row0 + c * RET_CHUNK, row0 + (c + 1) * RET_CHUNK)
        cross, state = _ret_cross(q_ref[rows, :].astype(F32), k_ref[rows, :].astype(F32),
                                  v_ref[rows, :].astype(BF16), state, k_scale, q_scale, chunk_decay)
        o_ref[rows, :] = cross
    state_ref[...] = state


def _ret_fwd_rows(lg_fwd, lg_bwd, q_ref, k_ref, v_ref, gate_ref, bwd_ref, gain_ref, o_ref, state_ref, row0):
    mask_f, k_scale, q_scale, chunk_decay = _decay_tables(lg_fwd, backward=False)
    mask = mask_f + _decay_tables(lg_bwd, backward=True)[0]
    state = state_ref[...]
    for c in range(RET_ROWS // RET_CHUNK):
        rows = slice(row0 + c * RET_CHUNK, row0 + (c + 1) * RET_CHUNK)
        q, k, vb = q_ref[rows, :].astype(F32), k_ref[rows, :].astype(F32), v_ref[rows, :].astype(BF16)
        s = lax.dot_general(q.astype(BF16), k.astype(BF16), (((1,), (1,)), ((), ())), preferred_element_type=F32)
        inner = jnp.dot((s * mask).astype(BF16), vb, preferred_element_type=F32)
        cross, state = _ret_cross(q, k, vb, state, k_scale, q_scale, chunk_decay)
        both = _rms_norm_rows(inner + cross + bwd_ref[rows, :], gain_ref[...])
        o_ref[rows, :] = (jax.nn.silu(gate_ref[rows, :].astype(F32)) * both).astype(o_ref.dtype)
    state_ref[...] = state


def _ret_bwd_kernel(ld_ref, q_ref, k_ref, v_ref, o_ref, state_ref):
    h, t = pl.program_id(1), pl.program_id(2)

    @pl.when(t == 0)
    def _():
        state_ref[...] = jnp.zeros_like(state_ref)

    _ret_bwd_rows(ld_ref[h], q_ref, k_ref, v_ref, o_ref, state_ref, 0)


def _ret_fwd_kernel(ldf_ref, ldb_ref, q_ref, k_ref, v_ref, gate_ref, bwd_ref, gain_ref, o_ref, state_ref):
    h, t = pl.program_id(1), pl.program_id(2)

    @pl.when(t == 0)
    def _():
        state_ref[...] = jnp.zeros_like(state_ref)

    _ret_fwd_rows(ldf_ref[h], ldb_ref[h], q_ref, k_ref, v_ref, gate_ref, bwd_ref, gain_ref, o_ref, state_ref, 0)


def _ret_fused_kernel(ldf_ref, ldb_ref, q_ref, k_ref, v_ref, gate_ref, gain_ref, o_ref, bwd_ref, state_ref, *, seq):
    h = pl.program_id(1)
    blocks = range(0, seq, RET_ROWS)
    state_ref[...] = jnp.zeros_like(state_ref)
    for row0 in reversed(blocks):
        _ret_bwd_rows(ldb_ref[h], q_ref, k_ref, v_ref, bwd_ref, state_ref, row0)
    state_ref[...] = jnp.zeros_like(state_ref)
    for row0 in blocks:
        _ret_fwd_rows(ldf_ref[h], ldb_ref[h], q_ref, k_ref, v_ref, gate_ref, bwd_ref, gain_ref, o_ref, state_ref, row0)


def _ret_specs(n_t, reverse):
    tmap = (lambda t: n_t - 1 - t) if reverse else (lambda t: t)
    blk = (None, RET_ROWS, RET_DIM)
    seg0 = 3 * ATTN_WIDTH // RET_DIM

    def col(seg):
        return pl.BlockSpec(blk, lambda b, h, t: (b, tmap(t), seg0 + seg * RET_HEADS + h))

    return blk, tmap, col


def _ret_bwd(p3, log_decay):
    B, S, _ = p3.shape
    n_t = S // RET_ROWS
    blk, tmap, col = _ret_specs(n_t, reverse=True)
    vmem = 2 * 4 * RET_ROWS * RET_DIM * 4 + RET_DIM * RET_DIM * 4 + 8 * MIB
    return pl.pallas_call(
        _ret_bwd_kernel,
        grid=(B, RET_HEADS, n_t),
        in_specs=[pl.BlockSpec(memory_space=pltpu.SMEM), col(0), col(1), col(2)],
        out_specs=pl.BlockSpec(blk, lambda b, h, t: (b, tmap(t), h)),
        out_shape=jax.ShapeDtypeStruct((B, S, RET_WIDTH), F32),
        scratch_shapes=[pltpu.VMEM((RET_DIM, RET_DIM), F32)],
        compiler_params=_params(("parallel", "parallel", "arbitrary"), vmem, claim_all=True),
        name="ret_bwd",
    )(log_decay, p3, p3, p3)


def _ret_fwd(p3, bwd, log_decay_fwd, log_decay_bwd, out_gain):
    B, S, _ = p3.shape
    n_t = S // RET_ROWS
    blk, tmap, col = _ret_specs(n_t, reverse=False)
    vmem = 2 * 6 * RET_ROWS * RET_DIM * 4 + RET_DIM * RET_DIM * 4 + 8 * MIB
    smem = pl.BlockSpec(memory_space=pltpu.SMEM)
    return pl.pallas_call(
        _ret_fwd_kernel,
        grid=(B, RET_HEADS, n_t),
        in_specs=[smem, smem, col(0), col(1), col(2), col(3),
                  pl.BlockSpec(blk, lambda b, h, t: (b, t, h)),
                  pl.BlockSpec((None, 1, RET_DIM), lambda b, h, t: (h, 0, 0))],
        out_specs=pl.BlockSpec(blk, lambda b, h, t: (b, t, h)),
        out_shape=jax.ShapeDtypeStruct((B, S, RET_WIDTH), BF16),
        scratch_shapes=[pltpu.VMEM((RET_DIM, RET_DIM), F32)],
        compiler_params=_params(("parallel", "parallel", "arbitrary"), vmem, claim_all=True),
        name="ret_fwd",
    )(log_decay_fwd, log_decay_bwd, p3, p3, p3, p3, bwd, out_gain)


def _retention(p3, log_decay_fwd, log_decay_bwd, out_gain):
    B, S, _ = p3.shape
    seq_bytes = S * RET_DIM * 4
    fused_vmem = 2 * 4 * seq_bytes // 2 + seq_bytes + 2 * seq_bytes // 2 + RET_DIM * RET_DIM * 4 + 6 * MIB
    if fused_vmem > V7X_SCOPED_VMEM_BYTES:
        bwd = _ret_bwd(p3, log_decay_bwd)
        return _ret_fwd(p3, bwd, log_decay_fwd, log_decay_bwd, out_gain)
    blk = (None, S, RET_DIM)
    seg0 = 3 * ATTN_WIDTH // RET_DIM
    col = lambda seg: pl.BlockSpec(blk, lambda b, h: (b, 0, seg0 + seg * RET_HEADS + h))
    smem = pl.BlockSpec(memory_space=pltpu.SMEM)
    return pl.pallas_call(
        functools.partial(_ret_fused_kernel, seq=S),
        grid=(B, RET_HEADS),
        in_specs=[smem, smem, col(0), col(1), col(2), col(3),
                  pl.BlockSpec((None, 1, RET_DIM), lambda b, h: (h, 0, 0))],
        out_specs=pl.BlockSpec(blk, lambda b, h: (b, 0, h)),
        out_shape=jax.ShapeDtypeStruct((B, S, RET_WIDTH), BF16),
        scratch_shapes=[pltpu.VMEM((S, RET_DIM), F32), pltpu.VMEM((RET_DIM, RET_DIM), F32)],
        compiler_params=_params(("parallel", "parallel"), fused_vmem, claim_all=True),
        name="ret_fused",
    )(log_decay_fwd, log_decay_bwd, p3, p3, p3, p3, out_gain)


def _mix_out_kernel(x_ref, a_ref, r_ref, w_ref, o_ref):
    acc = jnp.dot(a_ref[...], w_ref[:ATTN_WIDTH, :], preferred_element_type=F32)
    acc += jnp.dot(r_ref[...], w_ref[ATTN_WIDTH:, :], preferred_element_type=F32)
    o_ref[...] = x_ref[...] + acc


def _mix_out(x, a, r, w_out):
    T, D = x.shape
    tm = MIX_ROWS
    vmem = 2 * (2 * tm * D * 4 + 2 * tm * SEG * 2) + D * D * 2 + tm * D * 4 + 4 * MIB
    return pl.pallas_call(
        _mix_out_kernel,
        grid=(T // tm,),
        in_specs=[
            pl.BlockSpec((tm, D), lambda i: (i, 0)),
            pl.BlockSpec((tm, ATTN_WIDTH), lambda i: (i, 0)),
            pl.BlockSpec((tm, RET_WIDTH), lambda i: (i, 0)),
            pl.BlockSpec((D, D), lambda i: (0, 0), pipeline_mode=pl.Buffered(1)),
        ],
        out_specs=pl.BlockSpec((tm, D), lambda i: (i, 0)),
        out_shape=jax.ShapeDtypeStruct((T, D), F32),
        compiler_params=_params(("parallel",), vmem, claim_all=True),
        name="mix_out",
    )(x, a, r, w_out)


_MIXER_WEIGHTS = ("w_in", "w_out")
_FFN1_WEIGHTS = ("ffn1_wg", "ffn1_wu", "ffn1_wd")
_FFN2_WEIGHTS = ("ffn2_wg", "ffn2_wu", "ffn2_wd")
_WEIGHT_PADS = {"w_in": (D_MODEL, IN_COLS), "w_out": (D_MODEL, D_MODEL),
                "ffn1_wg": (D_MODEL, D_FF_PAD), "ffn1_wu": (D_MODEL, D_FF_PAD), "ffn1_wd": (D_FF_PAD, D_MODEL),
                "ffn2_wg": (D_MODEL, D_FF_PAD), "ffn2_wu": (D_MODEL, D_FF_PAD), "ffn2_wd": (D_FF_PAD, D_MODEL)}


def _trunk(x3, layers, f32_weights, bf16_weights):
    B, S, D = x3.shape
    T = B * S
    x = x3.reshape(T, D)
    attn_tables, ret_tables = _rope_tables(S, ATTN_HEAD_DIM), _rope_tables(S, RET_DIM)

    def ffn_carrying(x, gain, own, layer, carried, carried_layer):
        names = [n for n in carried if carried_layer < len(layers) and (n, carried_layer) not in bf16_weights]
        casts = [_CarriedCast(f32_weights[n], carried_layer, *_WEIGHT_PADS[n]) for n in names]
        y, cast = _ffn(x, gain, *[bf16_weights[(n, layer)] for n in own], casts=casts)
        bf16_weights.update({(n, carried_layer): w for n, w in zip(names, cast)})
        return y

    for l, lw in enumerate(layers):
        x = ffn_carrying(x, lw["ffn1_norm"], _FFN1_WEIGHTS, l, _MIXER_WEIGHTS + _FFN2_WEIGHTS, l)
        proj = _proj(x, lw["mix_norm"], bf16_weights[("w_in", l)], ret_tables, S)
        p3 = proj.reshape(B, S, IN_COLS)
        a = _attn(p3, lw["q_gain"], lw["k_gain"], attn_tables)
        r = _retention(p3, lw["ld_fwd"], lw["ld_bwd"], lw["ret_gain"])
        x = _mix_out(x, a.reshape(T, ATTN_WIDTH), r.reshape(T, RET_WIDTH), bf16_weights[("w_out", l)])
        x = ffn_carrying(x, lw["ffn2_norm"], _FFN2_WEIGHTS, l, _FFN1_WEIGHTS, l + 1)
    return x.reshape(B, S, D)


def kernel(x_prompt, x_sample, ffn1_norm, ffn1_w_gate, ffn1_w_up, ffn1_w_down, mix_norm, w_in, attn_q_norm, attn_k_norm, ret_log_decay_fwd, ret_log_decay_bwd, ret_out_norm, w_out, ffn2_norm, ffn2_w_gate, ffn2_w_up, ffn2_w_down):
    depth = w_in.shape[0]
    layers = []
    for l in range(depth):
        layers.append({
            "ffn1_norm": ffn1_norm[l][None, :].astype(F32),
            "mix_norm": mix_norm[l][None, :].astype(F32),
            "q_gain": attn_q_norm[l][None, :].astype(F32),
            "k_gain": attn_k_norm[l][None, :].astype(F32),
            "ld_fwd": ret_log_decay_fwd[l].astype(F32),
            "ld_bwd": ret_log_decay_bwd[l].astype(F32),
            "ret_gain": ret_out_norm[l][:, None, :].astype(F32),
            "ffn2_norm": ffn2_norm[l][None, :].astype(F32),
        })
    f32_weights = {"w_in": w_in, "w_out": w_out,
                   "ffn1_wg": ffn1_w_gate, "ffn1_wu": ffn1_w_up, "ffn1_wd": ffn1_w_down,
                   "ffn2_wg": ffn2_w_gate, "ffn2_wu": ffn2_w_up, "ffn2_wd": ffn2_w_down}
    bf16_weights = {(n, 0): _cast_layer(f32_weights[n], 0, *_WEIGHT_PADS[n]) for n in _FFN1_WEIGHTS}
    y_sample = _trunk(x_sample, layers, f32_weights, bf16_weights)
    y_prompt = _trunk(x_prompt, layers, f32_weights, bf16_weights)
    return (y_prompt, y_sample)
```
